```python
import math
import jax, jax.numpy as jnp
from jax import lax
import numpy as np

D_MODEL = 1024
BATCH = 8
SEQ = 8192
DEPTH = 1

ATT_HEADS = 8
ATT_QK_DIM = 64
ATT_V_DIM = 2 * ATT_QK_DIM
ATT_QK_WIDTH = ATT_HEADS * 2 * ATT_QK_DIM
ATT_WIDTH = ATT_HEADS * ATT_V_DIM
Q_BLOCK = 128
REL_BUCKETS = 32
REL_MAX_DIST = 128
SSM_INNER = 2 * D_MODEL
SSM_HEAD_DIM = 64
SSM_HEADS = SSM_INNER // SSM_HEAD_DIM
SSM_GROUPS = 8
SSM_HEADS_PER_GROUP = SSM_HEADS // SSM_GROUPS
SSM_STATE = 128
SSM_CONV = 4
SSM_CHUNK = 256
SSM_CONV_DIM = SSM_INNER + 2 * SSM_GROUPS * SSM_STATE
N_BRANCH = 2
EPS = 1e-6
SPLITS = (ATT_QK_WIDTH, ATT_QK_WIDTH, ATT_WIDTH, ATT_WIDTH,
          SSM_INNER, SSM_CONV_DIM, SSM_HEADS, N_BRANCH * D_MODEL)
D_IN_PROJ = sum(SPLITS)
SPLIT_IDX = [int(v) for v in np.cumsum(SPLITS)[:-1]]

kernel_name = "hybrid_diffattn_mamba2_gated_parallel"


def lambda_init_fn(layer_idx):
    return 0.8 - 0.6 * math.exp(-0.3 * layer_idx)


def rmsnorm(x, g):
    xf = x.astype(jnp.float32)
    y = xf * lax.rsqrt(jnp.mean(xf * xf, axis=-1, keepdims=True) + EPS)
    return (y * g.astype(jnp.float32)).astype(x.dtype)


def group_rmsnorm(y, g, groups):
    shape = y.shape
    yg = y.reshape(*shape[:-1], groups, shape[-1] // groups).astype(jnp.float32)
    yg = yg * lax.rsqrt(jnp.mean(yg * yg, axis=-1, keepdims=True) + EPS)
    return (yg.reshape(shape) * g.astype(jnp.float32)).astype(y.dtype)


def t5_bucket(n):
    max_exact = REL_BUCKETS // 2
    nf = jnp.maximum(n, 1).astype(jnp.float32)
    large = max_exact + (jnp.log(nf / max_exact) / math.log(REL_MAX_DIST / max_exact)
                         * (REL_BUCKETS - max_exact)).astype(jnp.int32)
    large = jnp.minimum(large, REL_BUCKETS - 1)
    return jnp.where(n < max_exact, n, large)


def diff_attention(q, k, v, rel_bias, lam):
    b, s, h = q.shape[0], q.shape[1], q.shape[2]
    nblk = s // Q_BLOCK
    qb = q.reshape(b, nblk, Q_BLOCK, h, 2, ATT_QK_DIM).swapaxes(0, 1)
    k_pos = jnp.arange(s)
    scale = ATT_QK_DIM ** -0.5

    def block(args):
        i, q_i = args
        q_pos = i * Q_BLOCK + jnp.arange(Q_BLOCK)
        dist = q_pos[:, None] - k_pos[None, :]
        causal = dist >= 0
        bias = rel_bias[t5_bucket(jnp.maximum(dist, 0))]
        bias = jnp.transpose(bias, (2, 0, 1)).astype(jnp.float32)
        logits = jnp.einsum('bqhcd,bkhcd->bhcqk', q_i, k).astype(jnp.float32) * scale
        logits = logits + bias[None, :, None]
        logits = jnp.where(causal, logits, -jnp.inf)
        p = jax.nn.softmax(logits, axis=-1)
        a = p[:, :, 0] - lam * p[:, :, 1]
        return jnp.einsum('bhqk,bkhe->bqhe', a.astype(v.dtype), v)

    out = lax.map(block, (jnp.arange(nblk), qb))
    return out.swapaxes(0, 1).reshape(b, s, h, ATT_V_DIM)


def causal_depthwise_conv(u, w, bias):
    y = lax.conv_general_dilated(u, w[:, None, :].astype(u.dtype), window_strides=(1,),
                                 padding=[(SSM_CONV - 1, 0)],
                                 dimension_numbers=('NWC', 'WIO', 'NWC'),
                                 feature_group_count=u.shape[-1])
    return y + bias


def ssd_chunked_scan(xs, dt, A, Bm, Cm):
    b, s = xs.shape[0], xs.shape[1]
    pad = (-s) % SSM_CHUNK
    sp = s + pad
    nc = sp // SSM_CHUNK
    G, HG, P, N, L = SSM_GROUPS, SSM_HEADS_PER_GROUP, SSM_HEAD_DIM, SSM_STATE, SSM_CHUNK
    X = xs * dt[..., None]
    Adt = dt.astype(jnp.float32) * A.astype(jnp.float32)
    X = jnp.pad(X, ((0, 0), (0, pad), (0, 0), (0, 0)))
    Adt = jnp.pad(Adt, ((0, 0), (0, pad), (0, 0)))
    Bp = jnp.pad(Bm, ((0, 0), (0, pad), (0, 0), (0, 0)))
    Cp = jnp.pad(Cm, ((0, 0), (0, pad), (0, 0), (0, 0)))
    Xc = X.reshape(b, nc, L, G, HG, P).swapaxes(0, 1)
    Ac = Adt.reshape(b, nc, L, SSM_HEADS).swapaxes(0, 1)
    Bc = Bp.reshape(b, nc, L, G, N).swapaxes(0, 1)
    Cc = Cp.reshape(b, nc, L, G, N).swapaxes(0, 1)
    idx = jnp.arange(L)
    tril = idx[:, None] >= idx[None, :]

    def step(state, inp):
        Xk, Ak, Bk, Ck = inp
        a_cs = jnp.cumsum(Ak, axis=1)
        a_cs = a_cs.transpose(0, 2, 1).reshape(b, G, HG, L)
        seg = a_cs[..., :, None] - a_cs[..., None, :]
        Lmat = jnp.exp(jnp.where(tril, seg, -jnp.inf))
        cb = jnp.einsum('blgn,bsgn->bgls', Ck, Bk)
        y_diag = jnp.einsum('bgls,bghls,bsghp->blghp', cb, Lmat, Xk)
        y_off = jnp.einsum('blgn,bghpn,bghl->blghp', Ck, state, jnp.exp(a_cs))
        decay_to_end = jnp.exp(a_cs[..., -1:] - a_cs)
        new_state = state * jnp.exp(a_cs[..., -1])[..., None, None] + \
            jnp.einsum('bsgn,bghs,bsghp->bghpn', Bk, decay_to_end, Xk)
        return new_state, (y_diag + y_off).astype(jnp.float32)

    state0 = jnp.zeros((b, G, HG, P, N), jnp.float32)
    _, y = lax.scan(step, state0, (Xc, Ac, Bc, Cc))
    y = y.swapaxes(0, 1).reshape(b, sp, SSM_HEADS, P)
    return y[:, :s]


def setup_inputs(seed: int = 0) -> dict:
    key = jax.random.key(seed)
    ks = jax.random.split(key, 20)
    f32 = jnp.float32
    x = jax.random.normal(ks[0], (BATCH, SEQ, D_MODEL), f32)
    g_pre = 1.0 + 0.05 * jax.random.normal(ks[1], (DEPTH, D_MODEL), f32)
    w_in = jax.random.normal(ks[2], (DEPTH, D_MODEL, D_IN_PROJ), f32) * D_MODEL ** -0.5
    att_lambda_q1 = 0.1 * jax.random.normal(ks[3], (DEPTH, ATT_QK_DIM), f32)
    att_lambda_k1 = 0.1 * jax.random.normal(ks[4], (DEPTH, ATT_QK_DIM), f32)
    att_lambda_q2 = 0.1 * jax.random.normal(ks[5], (DEPTH, ATT_QK_DIM), f32)
    att_lambda_k2 = 0.1 * jax.random.normal(ks[6], (DEPTH, ATT_QK_DIM), f32)
    att_subln_g = 1.0 + 0.05 * jax.random.normal(ks[7], (DEPTH, ATT_V_DIM), f32)
    rel_bias = 0.5 * jax.random.normal(ks[8], (REL_BUCKETS, ATT_HEADS), f32)
    conv_w = jax.random.normal(ks[9], (DEPTH, SSM_CONV, SSM_CONV_DIM), f32) * SSM_CONV ** -0.5
    conv_b = 0.02 * jax.random.normal(ks[10], (DEPTH, SSM_CONV_DIM), f32)
    u = jax.random.uniform(ks[11], (DEPTH, SSM_HEADS), f32)
    dt0 = jnp.exp(u * (math.log(0.1) - math.log(0.001)) + math.log(0.001))
    dt_bias = dt0 + jnp.log(-jnp.expm1(-dt0))
    a_log = jnp.log(jax.random.uniform(ks[12], (DEPTH, SSM_HEADS), f32, 1.0, 16.0))
    d_skip = 1.0 + 0.1 * jax.random.normal(ks[13], (DEPTH, SSM_HEADS), f32)
    ssm_norm_g = 1.0 + 0.05 * jax.random.normal(ks[14], (DEPTH, SSM_INNER), f32)
    w_att_proj = jax.random.normal(ks[15], (DEPTH, ATT_WIDTH, D_MODEL), f32) * ATT_WIDTH ** -0.5
    w_ssm_proj = jax.random.normal(ks[16], (DEPTH, SSM_INNER, D_MODEL), f32) * SSM_INNER ** -0.5
    w_out = jax.random.normal(ks[17], (DEPTH, D_MODEL, D_MODEL), f32) * D_MODEL ** -0.5
    g_post = 1.0 + 0.05 * jax.random.normal(ks[18], (DEPTH, D_MODEL), f32)
    return {"x": x, "g_pre": g_pre, "w_in": w_in,
            "att_lambda_q1": att_lambda_q1, "att_lambda_k1": att_lambda_k1,
            "att_lambda_q2": att_lambda_q2, "att_lambda_k2": att_lambda_k2,
            "att_subln_g": att_subln_g, "rel_bias": rel_bias,
            "conv_w": conv_w, "conv_b": conv_b, "dt_bias": dt_bias, "a_log": a_log,
            "d_skip": d_skip, "ssm_norm_g": ssm_norm_g,
            "w_att_proj": w_att_proj, "w_ssm_proj": w_ssm_proj, "w_out": w_out,
            "g_post": g_post}


def reference(x, g_pre, w_in, att_lambda_q1, att_lambda_k1, att_lambda_q2, att_lambda_k2,
              att_subln_g, rel_bias, conv_w, conv_b, dt_bias, a_log, d_skip, ssm_norm_g,
              w_att_proj, w_ssm_proj, w_out, g_post):
    b, s, _ = x.shape
    for layer in range(DEPTH):
        lambda_init = lambda_init_fn(layer)
        h = rmsnorm(x, g_pre[layer])
        proj = h @ w_in[layer]
        q, k, v, g_att, z, xbc, dt_raw, merge_logits = jnp.split(proj, SPLIT_IDX, axis=-1)

        q = q.reshape(b, s, ATT_HEADS, 2, ATT_QK_DIM)
        k = k.reshape(b, s, ATT_HEADS, 2, ATT_QK_DIM)
        v = v.reshape(b, s, ATT_HEADS, ATT_V_DIM)
        lam = (jnp.exp(jnp.sum(att_lambda_q1[layer].astype(jnp.float32) * att_lambda_k1[layer].astype(jnp.float32)))
               - jnp.exp(jnp.sum(att_lambda_q2[layer].astype(jnp.float32) * att_lambda_k2[layer].astype(jnp.float32)))
               + lambda_init)
        o = diff_attention(q, k, v, rel_bias, lam)
        o = rmsnorm(o, att_subln_g[layer]) * (1.0 - lambda_init)
        o = o.reshape(b, s, ATT_WIDTH) * jax.nn.silu(g_att)
        y_att = o @ w_att_proj[layer]

        xbc = jax.nn.silu(causal_depthwise_conv(xbc, conv_w[layer], conv_b[layer]))
        xs, Bm, Cm = jnp.split(xbc, [SSM_INNER, SSM_INNER + SSM_GROUPS * SSM_STATE], axis=-1)
        xs = xs.reshape(b, s, SSM_HEADS, SSM_HEAD_DIM)
        Bm = Bm.reshape(b, s, SSM_GROUPS, SSM_STATE)
        Cm = Cm.reshape(b, s, SSM_GROUPS, SSM_STATE)
        dt = jax.nn.softplus((dt_raw + dt_bias[layer]).astype(jnp.float32))
        A = -jnp.exp(a_log[layer].astype(jnp.float32))
        y = ssd_chunked_scan(xs, dt.astype(xs.dtype), A, Bm, Cm)
        y = y + xs.astype(jnp.float32) * d_skip[layer].astype(jnp.float32)[:, None]
        y = y.reshape(b, s, SSM_INNER).astype(x.dtype) * jax.nn.silu(z)
        y = group_rmsnorm(y, ssm_norm_g[layer], SSM_GROUPS)
        y_ssm = y @ w_ssm_proj[layer]

        gate_att, gate_ssm = jnp.split(merge_logits, 2, axis=-1)
        mixed = jax.nn.sigmoid(gate_att) * y_att + jax.nn.sigmoid(gate_ssm) * y_ssm
        out = mixed @ w_out[layer]
        x = x + rmsnorm(out, g_post[layer])
    return x
```

```python
import functools
import math

import numpy as np
import jax
import jax.numpy as jnp
from jax import lax
from jax.experimental import pallas as pl
from jax.experimental.pallas import tpu as pltpu

D_MODEL = 1024
ATT_HEADS = 8
ATT_QK_DIM = 64
ATT_V_DIM = 2 * ATT_QK_DIM
ATT_WIDTH = ATT_HEADS * ATT_V_DIM
REL_BUCKETS = 32
REL_MAX_DIST = 128
SSM_INNER = 2 * D_MODEL
SSM_HEAD_DIM = 64
SSM_HEADS = SSM_INNER // SSM_HEAD_DIM
SSM_GROUPS = 8
SSM_HEADS_PER_GROUP = SSM_HEADS // SSM_GROUPS
SSM_STATE = 128
SSM_CONV = 4
SSM_CHUNK = 256
SSM_CONV_DIM = SSM_INNER + 2 * SSM_GROUPS * SSM_STATE
EPS = 1e-6
LAMBDA_INIT = 0.8 - 0.6 * math.exp(-0.3 * 0)

LANES = 128
SUBLANES = 8
VMEM_LIMIT_BYTES = 56 * 1024 * 1024
LOG2E = math.log2(math.e)
NEG_BIG = -1e30

COL_Q = 0
COL_K = COL_Q + ATT_WIDTH
COL_V = COL_K + ATT_WIDTH
COL_GATT = COL_V + ATT_WIDTH
COL_XBC = COL_GATT + ATT_WIDTH
COL_Z = COL_XBC + SSM_CONV_DIM
COL_GATE = COL_Z + SSM_INNER
PROJ_COLS = COL_GATE + 2 * D_MODEL

ATT_BLOCK = 512
PROJ_TM = 1024
PROJ_TN = 512
OUT_TM = 512


def _t5_bucket_thresholds():
    max_exact = REL_BUCKETS // 2
    n = np.arange(0, 4 * REL_MAX_DIST)
    nf = np.maximum(n, 1).astype(np.float32)
    large = max_exact + (np.log(nf / np.float32(max_exact)) / np.float32(math.log(REL_MAX_DIST / max_exact))
                         * np.float32(REL_BUCKETS - max_exact)).astype(np.int32)
    large = np.minimum(large, REL_BUCKETS - 1)
    bucket = np.where(n < max_exact, n, large)
    assert np.all(np.diff(bucket) >= 0)
    thr = [int(np.argmax(bucket >= b)) for b in range(REL_BUCKETS)]
    assert bucket[thr[-1]] == REL_BUCKETS - 1
    return thr


_BUCKET_THR = _t5_bucket_thresholds()


def _in_proj_kernel(x_ref, g_ref, w_ref, wdt_ref, scale_ref, o_ref, dt_ref, h_scr):
    @pl.when(pl.program_id(1) == 0)
    def _():
        xf = x_ref[...]
        ms = jnp.mean(xf * xf, axis=-1, keepdims=True)
        h = (xf * lax.rsqrt(ms + EPS) * g_ref[...]).astype(jnp.bfloat16)
        h_scr[...] = h
        dt_ref[...] = jnp.dot(h, wdt_ref[...], preferred_element_type=jnp.float32)

    acc = jnp.dot(h_scr[...], w_ref[...], preferred_element_type=jnp.float32)
    o_ref[...] = (acc * scale_ref[...]).astype(o_ref.dtype)


def _in_proj(x2, g_pre, w_main, w_dt, col_scale):
    m = x2.shape[0]
    tm = min(PROJ_TM, m)
    tn = PROJ_TN
    return pl.pallas_call(
        _in_proj_kernel,
        grid=(m // tm, PROJ_COLS // tn),
        in_specs=[
            pl.BlockSpec((tm, D_MODEL), lambda i, j: (i, 0)),
            pl.BlockSpec((1, D_MODEL), lambda i, j: (0, 0)),
            pl.BlockSpec((D_MODEL, tn), lambda i, j: (0, j)),
            pl.BlockSpec((D_MODEL, LANES), lambda i, j: (0, 0)),
            pl.BlockSpec((1, tn), lambda i, j: (0, j)),
        ],
        out_specs=[
            pl.BlockSpec((tm, tn), lambda i, j: (i, j)),
            pl.BlockSpec((tm, LANES), lambda i, j: (i, 0)),
        ],
        out_shape=[
            jax.ShapeDtypeStruct((m, PROJ_COLS), jnp.bfloat16),
            jax.ShapeDtypeStruct((m, LANES), jnp.float32),
        ],
        scratch_shapes=[pltpu.VMEM((tm, D_MODEL), jnp.bfloat16)],
        compiler_params=pltpu.CompilerParams(
            dimension_semantics=("arbitrary", "arbitrary"),
            vmem_limit_bytes=VMEM_LIMIT_BYTES),
        name="in_proj",
    )(x2, g_pre, w_main, w_dt, col_scale)


def _bias_kernel(rb_ref, o_ref, *, t):
    h = pl.program_id(0)
    row = lax.broadcasted_iota(jnp.int32, (t, t), 0)
    col = lax.broadcasted_iota(jnp.int32, (t, t), 1)
    far = rb_ref[REL_BUCKETS - 1, h]
    for d in range(2):
        dist = d * t + row - col
        val = jnp.full((t, t), rb_ref[0, h], jnp.float32)
        for b in range(1, REL_BUCKETS):
            val = jnp.where(dist >= _BUCKET_THR[b], rb_ref[b, h], val)
        val = (val - far) * LOG2E
        if d == 0:
            val = jnp.where(dist >= 0, val, NEG_BIG)
        o_ref[d] = val


def _bias_tiles(rel_bias, t):
    assert t + 1 >= _BUCKET_THR[-1]
    return pl.pallas_call(
        functools.partial(_bias_kernel, t=t),
        grid=(ATT_HEADS,),
        in_specs=[pl.BlockSpec(memory_space=pltpu.SMEM)],
        out_specs=pl.BlockSpec((None, 2, t, t), lambda h: (h, 0, 0, 0)),
        out_shape=jax.ShapeDtypeStruct((ATT_HEADS, 2, t, t), jnp.float32),
        compiler_params=pltpu.CompilerParams(dimension_semantics=("arbitrary",)),
        name="t5_bias",
    )(rel_bias)


def _attn_kernel(q_ref, k_ref, v_ref, gatt_ref, bias_ref, subln_ref, lq1_ref, lk1_ref, lq2_ref, lk2_ref,
                 o_ref, vaug_scr, m_scr, acc_scr, *, t):
    qi = pl.program_id(2)
    dv = ATT_V_DIM

    @pl.when(qi == 0)
    def _():
        vaug_scr[:, :dv] = v_ref[...]
        vaug_scr[:, dv:] = jnp.ones((v_ref.shape[0], dv), v_ref.dtype)

    q = q_ref[...]
    lane = lax.broadcasted_iota(jnp.int32, q.shape, 1)
    zero = jnp.zeros_like(q)
    qs = jnp.concatenate([jnp.where(lane < ATT_QK_DIM, q, zero),
                          jnp.where(lane >= ATT_QK_DIM, q, zero)], axis=0)

    m_scr[...] = jnp.full(m_scr.shape, -jnp.inf, jnp.float32)
    acc_scr[...] = jnp.zeros(acc_scr.shape, jnp.float32)

    def step(j, bias):
        start = pl.multiple_of(j * t, t)
        k = k_ref[pl.ds(start, t), :]
        s = lax.dot_general(qs, k, (((1,), (1,)), ((), ())),
                            preferred_element_type=jnp.float32)
        if bias is not None:
            s = (s.reshape(2, t, t) + bias[None]).reshape(2 * t, t)
        m_prev = m_scr[...]
        m_new = jnp.maximum(m_prev, jnp.max(s, axis=-1, keepdims=True))
        alpha = jnp.exp2(m_prev - m_new)
        p = jnp.exp2(s - m_new[:, :1]).astype(jnp.bfloat16)
        pv = jnp.dot(p, vaug_scr[pl.ds(start, t), :], preferred_element_type=jnp.float32)
        acc_scr[...] = acc_scr[...] * jnp.concatenate([alpha, alpha], axis=1) + pv
        m_scr[...] = m_new

    def far_body(j, carry):
        step(j, None)
        return carry

    lax.fori_loop(0, jnp.maximum(qi - 1, 0), far_body, 0)

    @pl.when(qi >= 1)
    def _():
        step(qi - 1, bias_ref[1])

    step(qi, bias_ref[0])

    acc = acc_scr[...]
    o = acc[:, :dv] / acc[:, dv:]
    lam = (jnp.exp(jnp.sum(lq1_ref[...] * lk1_ref[...], axis=-1, keepdims=True))
           - jnp.exp(jnp.sum(lq2_ref[...] * lk2_ref[...], axis=-1, keepdims=True))
           + LAMBDA_INIT)
    o = o[:t] - lam * o[t:]
    ms = jnp.mean(o * o, axis=-1, keepdims=True)
    o = o * lax.rsqrt(ms + EPS) * subln_ref[...] * (1.0 - LAMBDA_INIT)
    g = gatt_ref[...].astype(jnp.float32)
    o = o * (g / (1.0 + jnp.exp(-g)))
    o_ref[...] = o.astype(o_ref.dtype)


def _attention(proj3, bias, subln_g, lq1, lk1, lq2, lk2, t):
    b, s, _ = proj3.shape
    hd = ATT_V_DIM
    kq, kk, kv, kg = COL_Q // hd, COL_K // hd, COL_V // hd, COL_GATT // hd
    vec = lambda n: pl.BlockSpec((1, n), lambda bi, h, qi: (0, 0))
    return pl.pallas_call(
        functools.partial(_attn_kernel, t=t),
        grid=(b, ATT_HEADS, s // t),
        in_specs=[
            pl.BlockSpec((None, t, hd), lambda bi, h, qi: (bi, qi, kq + h)),
            pl.BlockSpec((None, s, hd), lambda bi, h, qi: (bi, 0, kk + h)),
            pl.BlockSpec((None, s, hd), lambda bi, h, qi: (bi, 0, kv + h)),
            pl.BlockSpec((None, t, hd), lambda bi, h, qi: (bi, qi, kg + h)),
            pl.BlockSpec((None, 2, t, t), lambda bi, h, qi: (h, 0, 0, 0)),
            vec(hd), vec(ATT_QK_DIM), vec(ATT_QK_DIM), vec(ATT_QK_DIM), vec(ATT_QK_DIM),
        ],
        out_specs=pl.BlockSpec((None, t, hd), lambda bi, h, qi: (bi, qi, h)),
        out_shape=jax.ShapeDtypeStruct((b, s, ATT_WIDTH), jnp.bfloat16),
        scratch_shapes=[
            pltpu.VMEM((s, 2 * hd), jnp.bfloat16),
            pltpu.VMEM((2 * t, LANES), jnp.float32),
            pltpu.VMEM((2 * t, 2 * hd), jnp.float32),
        ],
        compiler_params=pltpu.CompilerParams(
            dimension_semantics=("arbitrary", "arbitrary", "arbitrary"),
            vmem_limit_bytes=VMEM_LIMIT_BYTES),
        name="diff_attn",
    )(proj3, proj3, proj3, proj3, bias, subln_g, lq1, lk1, lq2, lk2)


def _silu(x):
    return x * (1.0 / (1.0 + jnp.exp(-x)))


def _ssd_kernel(xbc_ref, z_ref, dt_ref, convw_ref, convb_ref, dtb_ref, alog_ref, dskip_ref, normg_ref,
                o_ref, ext_scr, xs_scr, b_scr, c_scr, y_scr, st_scr):
    c = pl.program_id(1)
    L = SSM_CHUNK
    P, N, G, HG = SSM_HEAD_DIM, SSM_STATE, SSM_GROUPS, SSM_HEADS_PER_GROUP
    halo = SUBLANES

    @pl.when(c == 0)
    def _():
        ext_scr[:halo, :] = jnp.zeros((halo, SSM_CONV_DIM), jnp.float32)
        st_scr[...] = jnp.zeros(st_scr.shape, jnp.float32)

    ct = 512
    for t0 in range(0, SSM_CONV_DIM, ct):
        cs = slice(t0, t0 + ct)
        ext_scr[halo:, cs] = xbc_ref[:, cs].astype(jnp.float32)
        acc = convb_ref[:, cs] + convw_ref[SSM_CONV - 1:SSM_CONV, cs] * ext_scr[halo:, cs]
        for tap in range(SSM_CONV - 1):
            off = halo - (SSM_CONV - 1) + tap
            acc = acc + convw_ref[tap:tap + 1, cs] * ext_scr[off:off + L, cs]
        act = _silu(acc)
        ext_scr[:halo, cs] = ext_scr[L:L + halo, cs]
        if t0 < SSM_INNER:
            xs_scr[:, cs] = act
        elif t0 < SSM_INNER + G * N:
            b_scr[:, t0 - SSM_INNER:t0 - SSM_INNER + ct] = act.astype(jnp.bfloat16)
        else:
            c_scr[:, t0 - SSM_INNER - G * N:t0 - SSM_INNER - G * N + ct] = act.astype(jnp.bfloat16)

    xdt = dt_ref[...] + dtb_ref[...]
    dtv = jnp.maximum(xdt, 0.0) + jnp.log1p(jnp.exp(-jnp.abs(xdt)))
    adt = dtv * (-jnp.exp(alog_ref[...]))
    row = lax.broadcasted_iota(jnp.int32, (L, L), 0)
    col = lax.broadcasted_iota(jnp.int32, (L, L), 1)
    tril = row >= col
    tri = jnp.where(tril, 1.0, 0.0).astype(jnp.bfloat16)
    hi = adt.astype(jnp.bfloat16)
    r1 = adt - hi.astype(jnp.float32)
    mid = r1.astype(jnp.bfloat16)
    lo = (r1 - mid.astype(jnp.float32)).astype(jnp.bfloat16)
    a_cs = (jnp.dot(tri, hi, preferred_element_type=jnp.float32)
            + jnp.dot(tri, mid, preferred_element_type=jnp.float32)
            + jnp.dot(tri, lo, preferred_element_type=jnp.float32))
    a_cs_t = a_cs.T
    a_last = a_cs[L - 1:L, :]
    exp_a = jnp.exp(a_cs)
    decay_end = jnp.exp(a_last - a_cs)
    exp_last = jnp.exp(a_last)

    for g in range(G):
        b_g = b_scr[:, g * N:(g + 1) * N]
        c_g = c_scr[:, g * N:(g + 1) * N]
        cb = lax.dot_general(c_g, b_g, (((1,), (1,)), ((), ())), preferred_element_type=jnp.float32)
        b_g_t = b_g.astype(jnp.float32).T.astype(jnp.bfloat16)
        st_g = st_scr[g]
        y_off = jnp.dot(c_g, st_g.astype(jnp.bfloat16), preferred_element_type=jnp.float32)
        for hh in range(HG):
            h = g * HG + hh
            hs = slice(h * P, (h + 1) * P)
            ls = slice(hh * P, (hh + 1) * P)
            a_col = a_cs[:, h:h + 1]
            seg = a_col - a_cs_t[h:h + 1, :]
            lmat = jnp.exp(jnp.where(tril, seg, -jnp.inf))
            mh = (cb * lmat).astype(jnp.bfloat16)
            xs_h = xs_scr[:, hs]
            x_h = xs_h * dtv[:, h:h + 1]
            y_h = jnp.dot(mh, x_h.astype(jnp.bfloat16), preferred_element_type=jnp.float32)
            y_h = y_h + y_off[:, ls] * exp_a[:, h:h + 1] + xs_h * dskip_ref[:, hs]
            y_scr[:, hs] = y_h
            xd_h = (x_h * decay_end[:, h:h + 1]).astype(jnp.bfloat16)
            upd = jnp.dot(b_g_t, xd_h, preferred_element_type=jnp.float32)
            st_scr[g, :, ls] = st_g[:, ls] * exp_last[:, h:h + 1] + upd

    gw = SSM_INNER // G
    for g in range(G):
        gs = slice(g * gw, (g + 1) * gw)
        zz = z_ref[:, gs].astype(jnp.float32)
        y = y_scr[:, gs] * _silu(zz)
        ms = jnp.mean(y * y, axis=-1, keepdims=True)
        o_ref[:, gs] = (y * lax.rsqrt(ms + EPS) * normg_ref[:, gs]).astype(o_ref.dtype)


def _ssd(proj3, dt3, conv_w, conv_b, dt_bias, a_log, d_skip_ch, norm_g):
    b, s, _ = proj3.shape
    L = SSM_CHUNK
    assert s % L == 0
    full = lambda shape: pl.BlockSpec(shape, lambda bi, c: (0,) * len(shape))
    return pl.pallas_call(
        _ssd_kernel,
        grid=(b, s // L),
        in_specs=[
            pl.BlockSpec((None, L, SSM_CONV_DIM), lambda bi, c: (bi, c, COL_XBC // SSM_CONV_DIM)),
            pl.BlockSpec((None, L, SSM_INNER), lambda bi, c: (bi, c, COL_Z // SSM_INNER)),
            pl.BlockSpec((None, L, LANES), lambda bi, c: (bi, c, 0)),
            full((SSM_CONV, SSM_CONV_DIM)), full((1, SSM_CONV_DIM)),
            full((1, LANES)), full((1, LANES)), full((1, SSM_INNER)), full((1, SSM_INNER)),
        ],
        out_specs=pl.BlockSpec((None, L, SSM_INNER), lambda bi, c: (bi, c, 0)),
        out_shape=jax.ShapeDtypeStruct((b, s, SSM_INNER), jnp.bfloat16),
        scratch_shapes=[
            pltpu.VMEM((L + SUBLANES, SSM_CONV_DIM), jnp.float32),
            pltpu.VMEM((L, SSM_INNER), jnp.float32),
            pltpu.VMEM((L, SSM_GROUPS * SSM_STATE), jnp.bfloat16),
            pltpu.VMEM((L, SSM_GROUPS * SSM_STATE), jnp.bfloat16),
            pltpu.VMEM((L, SSM_INNER), jnp.float32),
            pltpu.VMEM((SSM_GROUPS, SSM_STATE, SSM_HEADS_PER_GROUP * SSM_HEAD_DIM), jnp.float32),
        ],
        compiler_params=pltpu.CompilerParams(
            dimension_semantics=("arbitrary", "arbitrary"),
            vmem_limit_bytes=VMEM_LIMIT_BYTES),
        name="ssd",
    )(proj3, proj3, dt3, conv_w, conv_b, dt_bias, a_log, d_skip_ch, norm_g)


def _out_kernel(x_ref, oatt_ref, yssm_ref, ga_ref, gs_ref, watt_ref, wssm_ref, wout_ref, gpost_ref, o_ref):
    y_att = jnp.dot(oatt_ref[...], watt_ref[...], preferred_element_type=jnp.float32)
    y_ssm = jnp.dot(yssm_ref[...], wssm_ref[...], preferred_element_type=jnp.float32)
    ga = ga_ref[...].astype(jnp.float32)
    gs = gs_ref[...].astype(jnp.float32)
    mixed = y_att / (1.0 + jnp.exp(-ga)) + y_ssm / (1.0 + jnp.exp(-gs))
    out = jnp.dot(mixed.astype(jnp.bfloat16), wout_ref[...], preferred_element_type=jnp.float32)
    ms = jnp.mean(out * out, axis=-1, keepdims=True)
    o_ref[...] = x_ref[...] + out * lax.rsqrt(ms + EPS) * gpost_ref[...]


def _out_proj(x2, o_att, y_ssm, proj2, w_att, w_ssm, w_out, g_post):
    m = x2.shape[0]
    tm = min(OUT_TM, m)
    const = lambda shape: pl.BlockSpec(shape, lambda i: (0, 0))
    return pl.pallas_call(
        _out_kernel,
        grid=(m // tm,),
        in_specs=[
            pl.BlockSpec((tm, D_MODEL), lambda i: (i, 0)),
            pl.BlockSpec((tm, ATT_WIDTH), lambda i: (i, 0)),
            pl.BlockSpec((tm, SSM_INNER), lambda i: (i, 0)),
            pl.BlockSpec((tm, D_MODEL), lambda i: (i, COL_GATE // D_MODEL)),
            pl.BlockSpec((tm, D_MODEL), lambda i: (i, COL_GATE // D_MODEL + 1)),
            const((ATT_WIDTH, D_MODEL)), const((SSM_INNER, D_MODEL)), const((D_MODEL, D_MODEL)),
            const((1, D_MODEL)),
        ],
        out_specs=pl.BlockSpec((tm, D_MODEL), lambda i: (i, 0)),
        out_shape=jax.ShapeDtypeStruct((m, D_MODEL), jnp.float32),
        compiler_params=pltpu.CompilerParams(
            dimension_semantics=("arbitrary",),
            vmem_limit_bytes=VMEM_LIMIT_BYTES),
        name="out_proj",
    )(x2, o_att, y_ssm, proj2, proj2, w_att, w_ssm, w_out, g_post)


def kernel(x, g_pre, w_in, att_lambda_q1, att_lambda_k1, att_lambda_q2, att_lambda_k2, att_subln_g, rel_bias,
           conv_w, conv_b, dt_bias, a_log, d_skip, ssm_norm_g, w_att_proj, w_ssm_proj, w_out, g_post):
    b, s, d = x.shape
    assert d == D_MODEL and g_pre.shape[0] == 1
    m = b * s
    f32, bf16 = jnp.float32, jnp.bfloat16
    t = min(ATT_BLOCK, s)
    assert s % t == 0 and s % SSM_CHUNK == 0

    w = w_in[0]
    o_q, o_k, o_v, o_g = 0, ATT_WIDTH, 2 * ATT_WIDTH, 3 * ATT_WIDTH
    o_z = 4 * ATT_WIDTH
    o_xbc = o_z + SSM_INNER
    o_dt = o_xbc + SSM_CONV_DIM
    o_gate = o_dt + SSM_HEADS
    w_main = jnp.concatenate([w[:, o_q:o_z], w[:, o_xbc:o_dt], w[:, o_z:o_xbc], w[:, o_gate:]], axis=1).astype(bf16)
    w_dt = jnp.pad(w[:, o_dt:o_gate], ((0, 0), (0, LANES - SSM_HEADS))).astype(bf16)
    col_scale = jnp.concatenate([jnp.full((1, ATT_WIDTH), ATT_QK_DIM ** -0.5 * LOG2E, f32),
                                 jnp.ones((1, PROJ_COLS - ATT_WIDTH), f32)], axis=1)

    x2 = x.reshape(m, d)
    proj, dt_raw = _in_proj(x2, g_pre, w_main, w_dt, col_scale)
    proj3 = proj.reshape(b, s, PROJ_COLS)
    dt3 = dt_raw.reshape(b, s, LANES)

    bias = _bias_tiles(rel_bias, t)
    o_att = _attention(proj3, bias, att_subln_g, att_lambda_q1, att_lambda_k1, att_lambda_q2, att_lambda_k2, t)

    pad_h = lambda v: jnp.pad(v, ((0, 0), (0, LANES - SSM_HEADS)))
    d_skip_ch = jnp.repeat(d_skip, SSM_HEAD_DIM, axis=1)
    y_ssm = _ssd(proj3, dt3, conv_w[0], conv_b, pad_h(dt_bias), pad_h(a_log), d_skip_ch, ssm_norm_g)

    out = _out_proj(x2, o_att.reshape(m, ATT_WIDTH), y_ssm.reshape(m, SSM_INNER), proj,
                    w_att_proj[0].astype(bf16), w_ssm_proj[0].astype(bf16), w_out[0].astype(bf16), g_post)
    return out.reshape(b, s, d)
```

```python
import functools
import math

import numpy as np
import jax
import jax.numpy as jnp
from jax import lax
from jax.experimental import pallas as pl
from jax.experimental.pallas import tpu as pltpu

D_MODEL = 1024
ATT_HEADS = 8
ATT_QK_DIM = 64
ATT_V_DIM = 2 * ATT_QK_DIM
ATT_WIDTH = ATT_HEADS * ATT_V_DIM
REL_BUCKETS = 32
REL_MAX_DIST = 128
SSM_INNER = 2 * D_MODEL
SSM_HEAD_DIM = 64
SSM_HEADS = SSM_INNER // SSM_HEAD_DIM
SSM_GROUPS = 8
SSM_HEADS_PER_GROUP = SSM_HEADS // SSM_GROUPS
SSM_STATE = 128
SSM_CONV = 4
SSM_CHUNK = 256
SSM_CONV_DIM = SSM_INNER + 2 * SSM_GROUPS * SSM_STATE
EPS = 1e-6
LAMBDA_INIT = 0.8 - 0.6 * math.exp(-0.3 * 0)

LANES = 128
SUBLANES = 8
VMEM_LIMIT_BYTES = 56 * 1024 * 1024
LOG2E = math.log2(math.e)
NEG_BIG = -1e30

COL_Q = 0
COL_K = COL_Q + ATT_WIDTH
COL_V = COL_K + ATT_WIDTH
COL_GATT = COL_V + ATT_WIDTH
COL_XBC = COL_GATT + ATT_WIDTH
COL_Z = COL_XBC + SSM_CONV_DIM
COL_GATE = COL_Z + SSM_INNER
PROJ_COLS = COL_GATE + 2 * D_MODEL

ATT_BLOCK = 512
N_BIAS_TILES = 3
PROJ_TM = 1024
PROJ_TN = 512
OUT_TM = 512


def _t5_bucket_thresholds():
    max_exact = REL_BUCKETS // 2
    n = np.arange(0, 4 * REL_MAX_DIST)
    nf = np.maximum(n, 1).astype(np.float32)
    large = max_exact + (np.log(nf / np.float32(max_exact)) / np.float32(math.log(REL_MAX_DIST / max_exact))
                         * np.float32(REL_BUCKETS - max_exact)).astype(np.int32)
    large = np.minimum(large, REL_BUCKETS - 1)
    bucket = np.where(n < max_exact, n, large)
    assert np.all(np.diff(bucket) >= 0)
    thr = [int(np.argmax(bucket >= b)) for b in range(REL_BUCKETS)]
    assert bucket[thr[-1]] == REL_BUCKETS - 1
    return thr


_BUCKET_THR = _t5_bucket_thresholds()


def _in_proj_kernel(x_ref, g_ref, w_ref, wdt_ref, scale_ref, o_ref, dt_ref, h_scr):
    @pl.when(pl.program_id(1) == 0)
    def _():
        xf = x_ref[...]
        ms = jnp.mean(xf * xf, axis=-1, keepdims=True)
        h = (xf * lax.rsqrt(ms + EPS) * g_ref[...]).astype(jnp.bfloat16)
        h_scr[...] = h
        dt_ref[...] = jnp.dot(h, wdt_ref[...], preferred_element_type=jnp.float32)

    acc = jnp.dot(h_scr[...], w_ref[...], preferred_element_type=jnp.float32)
    o_ref[...] = (acc * scale_ref[...]).astype(o_ref.dtype)


def _in_proj(x2, g_pre, w_main, w_dt, col_scale):
    m = x2.shape[0]
    tm = min(PROJ_TM, m)
    tn = PROJ_TN
    return pl.pallas_call(
        _in_proj_kernel,
        grid=(m // tm, PROJ_COLS // tn),
        in_specs=[
            pl.BlockSpec((tm, D_MODEL), lambda i, j: (i, 0)),
            pl.BlockSpec((1, D_MODEL), lambda i, j: (0, 0)),
            pl.BlockSpec((D_MODEL, tn), lambda i, j: (0, j)),
            pl.BlockSpec((D_MODEL, LANES), lambda i, j: (0, 0)),
            pl.BlockSpec((1, tn), lambda i, j: (0, j)),
        ],
        out_specs=[
            pl.BlockSpec((tm, tn), lambda i, j: (i, j)),
            pl.BlockSpec((tm, LANES), lambda i, j: (i, 0)),
        ],
        out_shape=[
            jax.ShapeDtypeStruct((m, PROJ_COLS), jnp.bfloat16),
            jax.ShapeDtypeStruct((m, LANES), jnp.float32),
        ],
        scratch_shapes=[pltpu.VMEM((tm, D_MODEL), jnp.bfloat16)],
        compiler_params=pltpu.CompilerParams(
            dimension_semantics=("arbitrary", "arbitrary"),
            vmem_limit_bytes=VMEM_LIMIT_BYTES),
        name="in_proj",
    )(x2, g_pre, w_main, w_dt, col_scale)


def _bias_kernel(rb_ref, o_ref, *, t):
    h = pl.program_id(0)
    row = lax.broadcasted_iota(jnp.int32, (t, t), 0)
    col = lax.broadcasted_iota(jnp.int32, (t, t), 1)
    far = rb_ref[REL_BUCKETS - 1, h]
    for d in range(N_BIAS_TILES):
        dist = d * t + row - col
        val = jnp.full((t, t), rb_ref[0, h], jnp.float32)
        for b in range(1, REL_BUCKETS):
            val = jnp.where(dist >= _BUCKET_THR[b], rb_ref[b, h], val)
        val = (val - far) * LOG2E
        if d == 0:
            val = jnp.where(dist >= 0, val, NEG_BIG)
        o_ref[d] = val


def _bias_tiles(rel_bias, t):
    assert t + 1 >= _BUCKET_THR[-1]
    return pl.pallas_call(
        functools.partial(_bias_kernel, t=t),
        grid=(ATT_HEADS,),
        in_specs=[pl.BlockSpec(memory_space=pltpu.SMEM)],
        out_specs=pl.BlockSpec((None, N_BIAS_TILES, t, t), lambda h: (h, 0, 0, 0)),
        out_shape=jax.ShapeDtypeStruct((ATT_HEADS, N_BIAS_TILES, t, t), jnp.float32),
        compiler_params=pltpu.CompilerParams(dimension_semantics=("arbitrary",)),
        name="t5_bias",
    )(rel_bias)


def _attn_kernel(q_ref, k_ref, v_ref, gatt_ref, bias_ref, subln_ref, lq1_ref, lk1_ref, lq2_ref, lk2_ref,
                 o_ref, vaug_scr, m_scr, acc_scr, s0_scr, s1_scr, p0_scr, p1_scr, al0_scr, al1_scr, *, t, nq):
    dv = ATT_V_DIM
    s_bufs, p_bufs, al_bufs = (s0_scr, s1_scr), (p0_scr, p1_scr), (al0_scr, al1_scr)
    steps = [(qi, j) for qi in range(nq) for j in range(qi + 1)]
    n_steps = len(steps)

    vaug_scr[:, :dv] = v_ref[...]
    vaug_scr[:, dv:] = jnp.ones((v_ref.shape[0], dv), v_ref.dtype)
    acc_scr[...] = jnp.zeros(acc_scr.shape, jnp.float32)
    m_scr[...] = jnp.full(m_scr.shape, -jnp.inf, jnp.float32)

    def rows(idx):
        if isinstance(idx, int):
            return pl.ds(idx * t, t)
        return pl.ds(pl.multiple_of(idx * t, t), t)

    def stage_a(qi, j, s_dst):
        q = q_ref[rows(qi), :]
        lane = lax.broadcasted_iota(jnp.int32, q.shape, 1)
        zero = jnp.zeros_like(q)
        qs = jnp.concatenate([jnp.where(lane < ATT_QK_DIM, q, zero),
                              jnp.where(lane >= ATT_QK_DIM, q, zero)], axis=0)
        s_dst[...] = lax.dot_general(qs, k_ref[rows(j), :], (((1,), (1,)), ((), ())),
                                     preferred_element_type=jnp.float32)

    def stage_b(qi, j, s_src, p_dst, al_dst):
        d = min(qi - j, N_BIAS_TILES - 1) if isinstance(qi, int) else jnp.minimum(qi - j, N_BIAS_TILES - 1)
        s = (s_src[...].reshape(2, t, t) + bias_ref[d][None]).reshape(2 * t, t)
        m_old = m_scr[...]
        if isinstance(j, int):
            m_prev = jnp.full(m_old.shape, -jnp.inf, jnp.float32) if j == 0 else m_old
        else:
            m_prev = jnp.where(j == 0, -jnp.inf, m_old)
        m_new = jnp.maximum(m_prev, jnp.max(s, axis=-1, keepdims=True))
        al_dst[...] = jnp.exp2(m_prev - m_new)
        p_dst[...] = jnp.exp2(s - m_new[:, :1]).astype(jnp.bfloat16)
        m_scr[...] = m_new

    def finalize(qi):
        acc = acc_scr[...]
        o = acc[:, :dv] / acc[:, dv:]
        lam = (jnp.exp(jnp.sum(lq1_ref[...] * lk1_ref[...], axis=-1, keepdims=True))
               - jnp.exp(jnp.sum(lq2_ref[...] * lk2_ref[...], axis=-1, keepdims=True))
               + LAMBDA_INIT)
        o = o[:t] - lam * o[t:]
        ms = jnp.mean(o * o, axis=-1, keepdims=True)
        o = o * lax.rsqrt(ms + EPS) * subln_ref[...] * (1.0 - LAMBDA_INIT)
        g = gatt_ref[rows(qi), :].astype(jnp.float32)
        o = o * (g / (1.0 + jnp.exp(-g)))
        o_ref[rows(qi), :] = o.astype(o_ref.dtype)

    def stage_c(qi, j, p_src, al_src):
        pv = jnp.dot(p_src[...], vaug_scr[rows(j), :], preferred_element_type=jnp.float32)
        al = al_src[...]
        acc_scr[...] = acc_scr[...] * jnp.concatenate([al, al], axis=1) + pv
        if isinstance(qi, int):
            if j == qi:
                finalize(qi)
        else:
            pl.when(j == qi)(lambda: finalize(qi))

    def iteration(it, par, a, b, c):
        if a is not None:
            stage_a(a[0], a[1], s_bufs[par])
        if b is not None:
            stage_b(b[0], b[1], s_bufs[1 - par], p_bufs[1 - par], al_bufs[1 - par])
        if c is not None:
            stage_c(c[0], c[1], p_bufs[par], al_bufs[par])

    def static_iteration(it):
        pick = lambda i: steps[i] if 0 <= i < n_steps else None
        iteration(it, it % 2, pick(it), pick(it - 1), pick(it - 2))

    def advance(qi, j):
        wrap = j == qi
        return jnp.where(wrap, qi + 1, qi), jnp.where(wrap, 0, j + 1)

    n_pairs = max(n_steps - 2, 0) // 2
    loop_end = 2 + 2 * n_pairs
    for it in range(min(2, n_steps + 2)):
        static_iteration(it)
    if n_pairs > 0:
        def pair_body(_, carry):
            a, b, c = carry[0:2], carry[2:4], carry[4:6]
            iteration(None, 0, a, b, c)
            a, b, c = advance(*a), a, b
            iteration(None, 1, a, b, c)
            a, b, c = advance(*a), a, b
            return (*a, *b, *c)

        init = tuple(jnp.int32(v) for i in (2, 1, 0) for v in steps[i])
        lax.fori_loop(0, n_pairs, pair_body, init)
    for it in range(loop_end, n_steps + 2):
        static_iteration(it)


def _attention(proj3, bias, subln_g, lq1, lk1, lq2, lk2, t):
    b, s, _ = proj3.shape
    hd = ATT_V_DIM
    kq, kk, kv, kg = COL_Q // hd, COL_K // hd, COL_V // hd, COL_GATT // hd
    vec = lambda n: pl.BlockSpec((1, n), lambda bi, h: (0, 0))
    head_cols = lambda first: pl.BlockSpec((None, s, hd), lambda bi, h: (bi, 0, first + h))
    return pl.pallas_call(
        functools.partial(_attn_kernel, t=t, nq=s // t),
        grid=(b, ATT_HEADS),
        in_specs=[
            head_cols(kq), head_cols(kk), head_cols(kv), head_cols(kg),
            pl.BlockSpec((None, N_BIAS_TILES, t, t), lambda bi, h: (h, 0, 0, 0)),
            vec(hd), vec(ATT_QK_DIM), vec(ATT_QK_DIM), vec(ATT_QK_DIM), vec(ATT_QK_DIM),
        ],
        out_specs=head_cols(0),
        out_shape=jax.ShapeDtypeStruct((b, s, ATT_WIDTH), jnp.bfloat16),
        scratch_shapes=[
            pltpu.VMEM((s, 2 * hd), jnp.bfloat16),
            pltpu.VMEM((2 * t, LANES), jnp.float32),
            pltpu.VMEM((2 * t, 2 * hd), jnp.float32),
            pltpu.VMEM((2 * t, t), jnp.float32), pltpu.VMEM((2 * t, t), jnp.float32),
            pltpu.VMEM((2 * t, t), jnp.bfloat16), pltpu.VMEM((2 * t, t), jnp.bfloat16),
            pltpu.VMEM((2 * t, LANES), jnp.float32), pltpu.VMEM((2 * t, LANES), jnp.float32),
        ],
        compiler_params=pltpu.CompilerParams(
            dimension_semantics=("arbitrary", "arbitrary"),
            vmem_limit_bytes=VMEM_LIMIT_BYTES),
        name="diff_attn",
    )(proj3, proj3, proj3, proj3, bias, subln_g, lq1, lk1, lq2, lk2)


def _silu(x):
    return x * (1.0 / (1.0 + jnp.exp(-x)))


def _ssd_kernel(xbc_ref, z_ref, dt_ref, convw_ref, convb_ref, dtb_ref, alog_ref, dskip_ref, normg_ref,
                o_ref, ext_scr, xs_scr, b_scr, c_scr, y_scr, st_scr):
    c = pl.program_id(1)
    L = SSM_CHUNK
    P, N, G, HG = SSM_HEAD_DIM, SSM_STATE, SSM_GROUPS, SSM_HEADS_PER_GROUP
    halo = SUBLANES

    @pl.when(c == 0)
    def _():
        ext_scr[:halo, :] = jnp.zeros((halo, SSM_CONV_DIM), jnp.float32)
        st_scr[...] = jnp.zeros(st_scr.shape, jnp.float32)

    ct = 512
    for t0 in range(0, SSM_CONV_DIM, ct):
        cs = slice(t0, t0 + ct)
        ext_scr[halo:, cs] = xbc_ref[:, cs].astype(jnp.float32)
        acc = convb_ref[:, cs] + convw_ref[SSM_CONV - 1:SSM_CONV, cs] * ext_scr[halo:, cs]
        for tap in range(SSM_CONV - 1):
            off = halo - (SSM_CONV - 1) + tap
            acc = acc + convw_ref[tap:tap + 1, cs] * ext_scr[off:off + L, cs]
        act = _silu(acc)
        ext_scr[:halo, cs] = ext_scr[L:L + halo, cs]
        if t0 < SSM_INNER:
            xs_scr[:, cs] = act
        elif t0 < SSM_INNER + G * N:
            b_scr[:, t0 - SSM_INNER:t0 - SSM_INNER + ct] = act.astype(jnp.bfloat16)
        else:
            c_scr[:, t0 - SSM_INNER - G * N:t0 - SSM_INNER - G * N + ct] = act.astype(jnp.bfloat16)

    xdt = dt_ref[...] + dtb_ref[...]
    dtv = jnp.maximum(xdt, 0.0) + jnp.log1p(jnp.exp(-jnp.abs(xdt)))
    adt = dtv * (-jnp.exp(alog_ref[...]))
    row = lax.broadcasted_iota(jnp.int32, (L, L), 0)
    col = lax.broadcasted_iota(jnp.int32, (L, L), 1)
    tril = row >= col
    tri = jnp.where(tril, 1.0, 0.0).astype(jnp.bfloat16)
    hi = adt.astype(jnp.bfloat16)
    r1 = adt - hi.astype(jnp.float32)
    mid = r1.astype(jnp.bfloat16)
    lo = (r1 - mid.astype(jnp.float32)).astype(jnp.bfloat16)
    a_cs = (jnp.dot(tri, hi, preferred_element_type=jnp.float32)
            + jnp.dot(tri, mid, preferred_element_type=jnp.float32)
            + jnp.dot(tri, lo, preferred_element_type=jnp.float32))
    a_cs_t = a_cs.T
    a_last = a_cs[L - 1:L, :]
    exp_a = jnp.exp(a_cs)
    decay_end = jnp.exp(a_last - a_cs)
    exp_last = jnp.exp(a_last)

    for g in range(G):
        b_g = b_scr[:, g * N:(g + 1) * N]
        c_g = c_scr[:, g * N:(g + 1) * N]
        cb = lax.dot_general(c_g, b_g, (((1,), (1,)), ((), ())), preferred_element_type=jnp.float32)
        b_g_t = b_g.astype(jnp.float32).T.astype(jnp.bfloat16)
        st_g = st_scr[g]
        y_off = jnp.dot(c_g, st_g.astype(jnp.bfloat16), preferred_element_type=jnp.float32)
        for hh in range(HG):
            h = g * HG + hh
            hs = slice(h * P, (h + 1) * P)
            ls = slice(hh * P, (hh + 1) * P)
            a_col = a_cs[:, h:h + 1]
            seg = a_col - a_cs_t[h:h + 1, :]
            lmat = jnp.exp(jnp.where(tril, seg, -jnp.inf))
            mh = (cb * lmat).astype(jnp.bfloat16)
            xs_h = xs_scr[:, hs]
            x_h = xs_h * dtv[:, h:h + 1]
            y_h = jnp.dot(mh, x_h.astype(jnp.bfloat16), preferred_element_type=jnp.float32)
            y_h = y_h + y_off[:, ls] * exp_a[:, h:h + 1] + xs_h * dskip_ref[:, hs]
            y_scr[:, hs] = y_h
            xd_h = (x_h * decay_end[:, h:h + 1]).astype(jnp.bfloat16)
            upd = jnp.dot(b_g_t, xd_h, preferred_element_type=jnp.float32)
            st_scr[g, :, ls] = st_g[:, ls] * exp_last[:, h:h + 1] + upd

    gw = SSM_INNER // G
    for g in range(G):
        gs = slice(g * gw, (g + 1) * gw)
        zz = z_ref[:, gs].astype(jnp.float32)
        y = y_scr[:, gs] * _silu(zz)
        ms = jnp.mean(y * y, axis=-1, keepdims=True)
        o_ref[:, gs] = (y * lax.rsqrt(ms + EPS) * normg_ref[:, gs]).astype(o_ref.dtype)


def _ssd(proj3, dt3, conv_w, conv_b, dt_bias, a_log, d_skip_ch, norm_g):
    b, s, _ = proj3.shape
    L = SSM_CHUNK
    assert s % L == 0
    full = lambda shape: pl.BlockSpec(shape, lambda bi, c: (0,) * len(shape))
    return pl.pallas_call(
        _ssd_kernel,
        grid=(b, s // L),
        in_specs=[
            pl.BlockSpec((None, L, SSM_CONV_DIM), lambda bi, c: (bi, c, COL_XBC // SSM_CONV_DIM)),
            pl.BlockSpec((None, L, SSM_INNER), lambda bi, c: (bi, c, COL_Z // SSM_INNER)),
            pl.BlockSpec((None, L, LANES), lambda bi, c: (bi, c, 0)),
            full((SSM_CONV, SSM_CONV_DIM)), full((1, SSM_CONV_DIM)),
            full((1, LANES)), full((1, LANES)), full((1, SSM_INNER)), full((1, SSM_INNER)),
        ],
        out_specs=pl.BlockSpec((None, L, SSM_INNER), lambda bi, c: (bi, c, 0)),
        out_shape=jax.ShapeDtypeStruct((b, s, SSM_INNER), jnp.bfloat16),
        scratch_shapes=[
            pltpu.VMEM((L + SUBLANES, SSM_CONV_DIM), jnp.float32),
            pltpu.VMEM((L, SSM_INNER), jnp.float32),
            pltpu.VMEM((L, SSM_GROUPS * SSM_STATE), jnp.bfloat16),
            pltpu.VMEM((L, SSM_GROUPS * SSM_STATE), jnp.bfloat16),
            pltpu.VMEM((L, SSM_INNER), jnp.float32),
            pltpu.VMEM((SSM_GROUPS, SSM_STATE, SSM_HEADS_PER_GROUP * SSM_HEAD_DIM), jnp.float32),
        ],
        compiler_params=pltpu.CompilerParams(
            dimension_semantics=("arbitrary", "arbitrary"),
            vmem_limit_bytes=VMEM_LIMIT_BYTES),
        name="ssd",
    )(proj3, proj3, dt3, conv_w, conv_b, dt_bias, a_log, d_skip_ch, norm_g)


def _out_kernel(x_ref, oatt_ref, yssm_ref, ga_ref, gs_ref, watt_ref, wssm_ref, wout_ref, gpost_ref, o_ref):
    y_att = jnp.dot(oatt_ref[...], watt_ref[...], preferred_element_type=jnp.float32)
    y_ssm = jnp.dot(yssm_ref[...], wssm_ref[...], preferred_element_type=jnp.float32)
    ga = ga_ref[...].astype(jnp.float32)
    gs = gs_ref[...].astype(jnp.float32)
    mixed = y_att / (1.0 + jnp.exp(-ga)) + y_ssm / (1.0 + jnp.exp(-gs))
    out = jnp.dot(mixed.astype(jnp.bfloat16), wout_ref[...], preferred_element_type=jnp.float32)
    ms = jnp.mean(out * out, axis=-1, keepdims=True)
    o_ref[...] = x_ref[...] + out * lax.rsqrt(ms + EPS) * gpost_ref[...]


def _out_proj(x2, o_att, y_ssm, proj2, w_att, w_ssm, w_out, g_post):
    m = x2.shape[0]
    tm = min(OUT_TM, m)
    const = lambda shape: pl.BlockSpec(shape, lambda i: (0, 0))
    return pl.pallas_call(
        _out_kernel,
        grid=(m // tm,),
        in_specs=[
            pl.BlockSpec((tm, D_MODEL), lambda i: (i, 0)),
            pl.BlockSpec((tm, ATT_WIDTH), lambda i: (i, 0)),
            pl.BlockSpec((tm, SSM_INNER), lambda i: (i, 0)),
            pl.BlockSpec((tm, D_MODEL), lambda i: (i, COL_GATE // D_MODEL)),
            pl.BlockSpec((tm, D_MODEL), lambda i: (i, COL_GATE // D_MODEL + 1)),
            const((ATT_WIDTH, D_MODEL)), const((SSM_INNER, D_MODEL)), const((D_MODEL, D_MODEL)),
            const((1, D_MODEL)),
        ],
        out_specs=pl.BlockSpec((tm, D_MODEL), lambda i: (i, 0)),
        out_shape=jax.ShapeDtypeStruct((m, D_MODEL), jnp.float32),
        compiler_params=pltpu.CompilerParams(
            dimension_semantics=("arbitrary",),
            vmem_limit_bytes=VMEM_LIMIT_BYTES),
        name="out_proj",
    )(x2, o_att, y_ssm, proj2, proj2, w_att, w_ssm, w_out, g_post)


def kernel(x, g_pre, w_in, att_lambda_q1, att_lambda_k1, att_lambda_q2, att_lambda_k2, att_subln_g, rel_bias,
           conv_w, conv_b, dt_bias, a_log, d_skip, ssm_norm_g, w_att_proj, w_ssm_proj, w_out, g_post):
    b, s, d = x.shape
    assert d == D_MODEL and g_pre.shape[0] == 1
    m = b * s
    f32, bf16 = jnp.float32, jnp.bfloat16
    t = min(ATT_BLOCK, s)
    assert s % t == 0 and s % SSM_CHUNK == 0

    w = w_in[0]
    o_q, o_k, o_v, o_g = 0, ATT_WIDTH, 2 * ATT_WIDTH, 3 * ATT_WIDTH
    o_z = 4 * ATT_WIDTH
    o_xbc = o_z + SSM_INNER
    o_dt = o_xbc + SSM_CONV_DIM
    o_gate = o_dt + SSM_HEADS
    w_main = jnp.concatenate([w[:, o_q:o_z], w[:, o_xbc:o_dt], w[:, o_z:o_xbc], w[:, o_gate:]], axis=1).astype(bf16)
    w_dt = jnp.pad(w[:, o_dt:o_gate], ((0, 0), (0, LANES - SSM_HEADS))).astype(bf16)
    col_scale = jnp.concatenate([jnp.full((1, ATT_WIDTH), ATT_QK_DIM ** -0.5 * LOG2E, f32),
                                 jnp.ones((1, PROJ_COLS - ATT_WIDTH), f32)], axis=1)

    x2 = x.reshape(m, d)
    proj, dt_raw = _in_proj(x2, g_pre, w_main, w_dt, col_scale)
    proj3 = proj.reshape(b, s, PROJ_COLS)
    dt3 = dt_raw.reshape(b, s, LANES)

    bias = _bias_tiles(rel_bias, t)
    o_att = _attention(proj3, bias, att_subln_g, att_lambda_q1, att_lambda_k1, att_lambda_q2, att_lambda_k2, t)

    pad_h = lambda v: jnp.pad(v, ((0, 0), (0, LANES - SSM_HEADS)))
    d_skip_ch = jnp.repeat(d_skip, SSM_HEAD_DIM, axis=1)
    y_ssm = _ssd(proj3, dt3, conv_w[0], conv_b, pad_h(dt_bias), pad_h(a_log), d_skip_ch, ssm_norm_g)

    out = _out_proj(x2, o_att.reshape(m, ATT_WIDTH), y_ssm.reshape(m, SSM_INNER), proj,
                    w_att_proj[0].astype(bf16), w_ssm_proj[0].astype(bf16), w_out[0].astype(bf16), g_post)
    return out.reshape(b, s, d)
```

```python
import functools
import math

import numpy as np
import jax
import jax.numpy as jnp
from jax import lax
from jax.experimental import pallas as pl
from jax.experimental.pallas import tpu as pltpu

D_MODEL = 1024
ATT_HEADS = 8
ATT_QK_DIM = 64
ATT_V_DIM = 2 * ATT_QK_DIM
ATT_WIDTH = ATT_HEADS * ATT_V_DIM
REL_BUCKETS = 32
REL_MAX_DIST = 128
SSM_INNER = 2 * D_MODEL
SSM_HEAD_DIM = 64
SSM_HEADS = SSM_INNER // SSM_HEAD_DIM
SSM_GROUPS = 8
SSM_HEADS_PER_GROUP = SSM_HEADS // SSM_GROUPS
SSM_STATE = 128
SSM_CONV = 4
SSM_CHUNK = 256
SSM_CONV_DIM = SSM_INNER + 2 * SSM_GROUPS * SSM_STATE
EPS = 1e-6
LAMBDA_INIT = 0.8 - 0.6 * math.exp(-0.3 * 0)

LANES = 128
SUBLANES = 8
VMEM_LIMIT_BYTES = 56 * 1024 * 1024
LOG2E = math.log2(math.e)
NEG_BIG = -1e30

COL_Q = 0
COL_K = COL_Q + ATT_WIDTH
COL_V = COL_K + ATT_WIDTH
COL_GATT = COL_V + ATT_WIDTH
COL_XBC = COL_GATT + ATT_WIDTH
COL_Z = COL_XBC + SSM_CONV_DIM
COL_GATE = COL_Z + SSM_INNER
PROJ_COLS = COL_GATE + 2 * D_MODEL

ATT_BLOCK = 512
N_BIAS_TILES = 3
ATT_SOFTMAX_ROWS = 64
ATT_UNROLL = 4
PROJ_TM = 1024
PROJ_TN = 1024
OUT_TM = 512


def _t5_bucket_thresholds():
    max_exact = REL_BUCKETS // 2
    n = np.arange(0, 4 * REL_MAX_DIST)
    nf = np.maximum(n, 1).astype(np.float32)
    large = max_exact + (np.log(nf / np.float32(max_exact)) / np.float32(math.log(REL_MAX_DIST / max_exact))
                         * np.float32(REL_BUCKETS - max_exact)).astype(np.int32)
    large = np.minimum(large, REL_BUCKETS - 1)
    bucket = np.where(n < max_exact, n, large)
    assert np.all(np.diff(bucket) >= 0)
    thr = [int(np.argmax(bucket >= b)) for b in range(REL_BUCKETS)]
    assert bucket[thr[-1]] == REL_BUCKETS - 1
    return thr


_BUCKET_THR = _t5_bucket_thresholds()


def _in_proj_kernel(x_ref, g_ref, w_ref, wdt_ref, scale_ref, o_ref, dt_ref, h_scr):
    @pl.when(pl.program_id(1) == 0)
    def _():
        xf = x_ref[...]
        ms = jnp.mean(xf * xf, axis=-1, keepdims=True)
        h = (xf * lax.rsqrt(ms + EPS) * g_ref[...]).astype(jnp.bfloat16)
        h_scr[...] = h
        dt_ref[...] = jnp.dot(h, wdt_ref[...], preferred_element_type=jnp.float32)

    acc = jnp.dot(h_scr[...], w_ref[...], preferred_element_type=jnp.float32)
    o_ref[...] = (acc * scale_ref[...]).astype(o_ref.dtype)


def _in_proj(x2, g_pre, w_main, w_dt, col_scale):
    m = x2.shape[0]
    tm = min(PROJ_TM, m)
    tn = PROJ_TN
    return pl.pallas_call(
        _in_proj_kernel,
        grid=(m // tm, PROJ_COLS // tn),
        in_specs=[
            pl.BlockSpec((tm, D_MODEL), lambda i, j: (i, 0)),
            pl.BlockSpec((1, D_MODEL), lambda i, j: (0, 0)),
            pl.BlockSpec((D_MODEL, tn), lambda i, j: (0, j)),
            pl.BlockSpec((D_MODEL, LANES), lambda i, j: (0, 0)),
            pl.BlockSpec((1, tn), lambda i, j: (0, j)),
        ],
        out_specs=[
            pl.BlockSpec((tm, tn), lambda i, j: (i, j)),
            pl.BlockSpec((tm, LANES), lambda i, j: (i, 0)),
        ],
        out_shape=[
            jax.ShapeDtypeStruct((m, PROJ_COLS), jnp.bfloat16),
            jax.ShapeDtypeStruct((m, LANES), jnp.float32),
        ],
        scratch_shapes=[pltpu.VMEM((tm, D_MODEL), jnp.bfloat16)],
        compiler_params=pltpu.CompilerParams(
            dimension_semantics=("arbitrary", "arbitrary"),
            vmem_limit_bytes=VMEM_LIMIT_BYTES),
        name="in_proj",
    )(x2, g_pre, w_main, w_dt, col_scale)


def _bias_kernel(rb_ref, o_ref, *, t):
    h = pl.program_id(0)
    row = lax.broadcasted_iota(jnp.int32, (t, t), 0)
    col = lax.broadcasted_iota(jnp.int32, (t, t), 1)
    far = rb_ref[REL_BUCKETS - 1, h]
    for d in range(N_BIAS_TILES):
        dist = d * t + row - col
        val = jnp.full((t, t), rb_ref[0, h], jnp.float32)
        for b in range(1, REL_BUCKETS):
            val = jnp.where(dist >= _BUCKET_THR[b], rb_ref[b, h], val)
        val = (val - far) * LOG2E
        if d == 0:
            val = jnp.where(dist >= 0, val, NEG_BIG)
        o_ref[d] = val


def _bias_tiles(rel_bias, t):
    assert t + 1 >= _BUCKET_THR[-1]
    return pl.pallas_call(
        functools.partial(_bias_kernel, t=t),
        grid=(ATT_HEADS,),
        in_specs=[pl.BlockSpec(memory_space=pltpu.SMEM)],
        out_specs=pl.BlockSpec((None, N_BIAS_TILES, t, t), lambda h: (h, 0, 0, 0)),
        out_shape=jax.ShapeDtypeStruct((ATT_HEADS, N_BIAS_TILES, t, t), jnp.float32),
        compiler_params=pltpu.CompilerParams(dimension_semantics=("arbitrary",)),
        name="t5_bias",
    )(rel_bias)


def _attn_kernel(q_ref, k_ref, v_ref, gatt_ref, bias_ref, subln_ref, lq1_ref, lk1_ref, lq2_ref, lk2_ref,
                 o_ref, vaug_scr, m_scr, acc_scr, s0_scr, s1_scr, p0_scr, p1_scr, al0_scr, al1_scr, *, t, nq):
    dv = ATT_V_DIM
    s_bufs, p_bufs, al_bufs = (s0_scr, s1_scr), (p0_scr, p1_scr), (al0_scr, al1_scr)
    steps = [(qi, j) for qi in range(nq) for j in range(qi + 1)]
    n_steps = len(steps)

    vaug_scr[:, :dv] = v_ref[...]
    vaug_scr[:, dv:] = jnp.ones((v_ref.shape[0], dv), v_ref.dtype)
    acc_scr[...] = jnp.zeros(acc_scr.shape, jnp.float32)
    m_scr[...] = jnp.full(m_scr.shape, -jnp.inf, jnp.float32)

    def rows(idx):
        if isinstance(idx, int):
            return pl.ds(idx * t, t)
        return pl.ds(pl.multiple_of(idx * t, t), t)

    def stage_a(qi, j, s_dst):
        q = q_ref[rows(qi), :]
        lane = lax.broadcasted_iota(jnp.int32, q.shape, 1)
        zero = jnp.zeros_like(q)
        qs = jnp.concatenate([jnp.where(lane < ATT_QK_DIM, q, zero),
                              jnp.where(lane >= ATT_QK_DIM, q, zero)], axis=0)
        s_dst[...] = lax.dot_general(qs, k_ref[rows(j), :], (((1,), (1,)), ((), ())),
                                     preferred_element_type=jnp.float32)

    def stage_b(qi, j, s_src, p_dst, al_dst):
        d = min(qi - j, N_BIAS_TILES - 1) if isinstance(qi, int) else jnp.minimum(qi - j, N_BIAS_TILES - 1)
        for r0 in range(0, 2 * t, ATT_SOFTMAX_ROWS):
            rs = slice(r0, r0 + ATT_SOFTMAX_ROWS)
            b0 = r0 % t
            s = s_src[rs, :] + bias_ref[d, b0:b0 + ATT_SOFTMAX_ROWS, :]
            m_old = m_scr[rs, :]
            if isinstance(j, int):
                m_prev = jnp.full(m_old.shape, -jnp.inf, jnp.float32) if j == 0 else m_old
            else:
                m_prev = jnp.where(j == 0, -jnp.inf, m_old)
            m_new = jnp.maximum(m_prev, jnp.max(s, axis=-1, keepdims=True))
            al_dst[rs, :] = jnp.exp2(m_prev - m_new)
            p_dst[rs, :] = jnp.exp2(s - m_new[:, :1]).astype(jnp.bfloat16)
            m_scr[rs, :] = m_new

    def finalize(qi):
        acc = acc_scr[qi % 2] if isinstance(qi, int) else acc_scr[qi & 1]
        o = acc[:, :dv] / acc[:, dv:]
        lam = (jnp.exp(jnp.sum(lq1_ref[...] * lk1_ref[...], axis=-1, keepdims=True))
               - jnp.exp(jnp.sum(lq2_ref[...] * lk2_ref[...], axis=-1, keepdims=True))
               + LAMBDA_INIT)
        o = o[:t] - lam * o[t:]
        ms = jnp.mean(o * o, axis=-1, keepdims=True)
        o = o * lax.rsqrt(ms + EPS) * subln_ref[...] * (1.0 - LAMBDA_INIT)
        g = gatt_ref[rows(qi), :].astype(jnp.float32)
        o = o * (g / (1.0 + jnp.exp(-g)))
        o_ref[rows(qi), :] = o.astype(o_ref.dtype)

    def stage_c(qi, j, p_src, al_src):
        slot = qi % 2 if isinstance(qi, int) else qi & 1
        pv = jnp.dot(p_src[...], vaug_scr[rows(j), :], preferred_element_type=jnp.float32)
        al = al_src[...]
        acc_scr[slot] = acc_scr[slot] * jnp.concatenate([al, al], axis=1) + pv

    def iteration(par, a, b, c):
        if a is not None:
            stage_a(a[0], a[1], s_bufs[par])
        if b is not None:
            stage_b(b[0], b[1], s_bufs[1 - par], p_bufs[1 - par], al_bufs[1 - par])
        if c is not None:
            stage_c(c[0], c[1], p_bufs[par], al_bufs[par])

    def static_iteration(it):
        pick = lambda i: steps[i] if 0 <= i < n_steps else None
        iteration(it % 2, pick(it), pick(it - 1), pick(it - 2))
        c = pick(it - 2)
        if c is not None and c[0] == c[1]:
            finalize(c[0])

    def advance(qi, j):
        wrap = j == qi
        return jnp.where(wrap, qi + 1, qi), jnp.where(wrap, 0, j + 1)

    unroll = ATT_UNROLL
    assert unroll % 2 == 0
    lo = 2 + unroll * (unroll - 1) // 2
    lo += lo % 2
    n_trips = max(n_steps - lo, 0) // unroll
    loop_end = lo + unroll * n_trips if n_trips > 0 else 0
    if n_trips > 0:
        for it in range(lo):
            static_iteration(it)

        def trip(_, carry):
            a, b, c = carry[0:2], carry[2:4], carry[4:6]
            done, done_q = jnp.bool_(False), jnp.int32(0)
            for k in range(unroll):
                iteration(k % 2, a, b, c)
                last = c[0] == c[1]
                done, done_q = done | last, jnp.where(last, c[0], done_q)
                a, b, c = advance(*a), a, b
            pl.when(done)(lambda: finalize(done_q))
            return (*a, *b, *c)

        init = tuple(jnp.int32(v) for i in (lo, lo - 1, lo - 2) for v in steps[i])
        lax.fori_loop(0, n_trips, trip, init)
    for it in range(loop_end, n_steps + 2):
        static_iteration(it)


def _attention(proj3, bias, subln_g, lq1, lk1, lq2, lk2, t):
    b, s, _ = proj3.shape
    hd = ATT_V_DIM
    kq, kk, kv, kg = COL_Q // hd, COL_K // hd, COL_V // hd, COL_GATT // hd
    vec = lambda n: pl.BlockSpec((1, n), lambda bi, h: (0, 0))
    head_cols = lambda first: pl.BlockSpec((None, s, hd), lambda bi, h: (bi, 0, first + h))
    return pl.pallas_call(
        functools.partial(_attn_kernel, t=t, nq=s // t),
        grid=(b, ATT_HEADS),
        in_specs=[
            head_cols(kq), head_cols(kk), head_cols(kv), head_cols(kg),
            pl.BlockSpec((None, N_BIAS_TILES, t, t), lambda bi, h: (h, 0, 0, 0)),
            vec(hd), vec(ATT_QK_DIM), vec(ATT_QK_DIM), vec(ATT_QK_DIM), vec(ATT_QK_DIM),
        ],
        out_specs=head_cols(0),
        out_shape=jax.ShapeDtypeStruct((b, s, ATT_WIDTH), jnp.bfloat16),
        scratch_shapes=[
            pltpu.VMEM((s, 2 * hd), jnp.bfloat16),
            pltpu.VMEM((2 * t, LANES), jnp.float32),
            pltpu.VMEM((2, 2 * t, 2 * hd), jnp.float32),
            pltpu.VMEM((2 * t, t), jnp.float32), pltpu.VMEM((2 * t, t), jnp.float32),
            pltpu.VMEM((2 * t, t), jnp.bfloat16), pltpu.VMEM((2 * t, t), jnp.bfloat16),
            pltpu.VMEM((2 * t, LANES), jnp.float32), pltpu.VMEM((2 * t, LANES), jnp.float32),
        ],
        compiler_params=pltpu.CompilerParams(
            dimension_semantics=("arbitrary", "arbitrary"),
            vmem_limit_bytes=VMEM_LIMIT_BYTES),
        name="diff_attn",
    )(proj3, proj3, proj3, proj3, bias, subln_g, lq1, lk1, lq2, lk2)


def _silu(x):
    h = 0.5 * x
    return h + h * jnp.tanh(h)


def _ssd_kernel(xbc_ref, z_ref, dt_ref, convw_ref, convb_ref, dtb_ref, alog_ref, dskip_ref, normg_ref,
                o_ref, u_scr, v2_scr, xs_scr, xsb_scr, b_scr, c_scr, y_scr, st_scr):
    c = pl.program_id(1)
    L = SSM_CHUNK
    P, N, G, HG = SSM_HEAD_DIM, SSM_STATE, SSM_GROUPS, SSM_HEADS_PER_GROUP
    halo = SUBLANES
    f32, bf16 = jnp.float32, jnp.bfloat16

    @pl.when(c == 0)
    def _():
        u_scr[:halo, :] = jnp.zeros((halo, SSM_CONV_DIM), f32)
        v2_scr[:halo, :] = jnp.zeros((halo, SSM_CONV_DIM), f32)
        st_scr[...] = jnp.zeros(st_scr.shape, f32)

    ct = 512
    for t0 in range(0, SSM_CONV_DIM, ct):
        cs = slice(t0, t0 + ct)
        w = [convw_ref[k:k + 1, cs] for k in range(SSM_CONV)]
        u = xbc_ref[:, cs].astype(f32)
        u_scr[halo:, cs] = u
        u1 = u_scr[halo - 1:halo - 1 + L, cs]
        v2_scr[halo:, cs] = w[1] * u + w[0] * u1
        act = _silu(convb_ref[:, cs] + w[3] * u + w[2] * u1 + v2_scr[halo - 2:halo - 2 + L, cs])
        u_scr[:halo, cs] = u_scr[L:L + halo, cs]
        v2_scr[:halo, cs] = v2_scr[L:L + halo, cs]
        if t0 < SSM_INNER:
            xs_scr[:, cs] = act
            xsb_scr[:, cs] = act.astype(bf16)
        elif t0 < SSM_INNER + G * N:
            b_scr[:, t0 - SSM_INNER:t0 - SSM_INNER + ct] = act.astype(bf16)
        else:
            c_scr[:, t0 - SSM_INNER - G * N:t0 - SSM_INNER - G * N + ct] = act.astype(bf16)

    xdt = dt_ref[...] + dtb_ref[...]
    dtv = jnp.maximum(xdt, 0.0) + jnp.log1p(jnp.exp(-jnp.abs(xdt)))
    adt = dtv * (-jnp.exp(alog_ref[...]))
    row = lax.broadcasted_iota(jnp.int32, (L, L), 0)
    col = lax.broadcasted_iota(jnp.int32, (L, L), 1)
    tril = row >= col
    tri = jnp.where(tril, 1.0, 0.0).astype(bf16)
    hi = adt.astype(bf16)
    r1 = adt - hi.astype(f32)
    mid = r1.astype(bf16)
    lo = (r1 - mid.astype(f32)).astype(bf16)
    a_cs = (jnp.dot(tri, hi, preferred_element_type=f32)
            + jnp.dot(tri, mid, preferred_element_type=f32)
            + jnp.dot(tri, lo, preferred_element_type=f32))
    a_last = a_cs[L - 1:L, :]
    exp_last = jnp.exp(a_last)
    a2 = a_cs * LOG2E
    src2_t = (a2 - jnp.log2(dtv)).T
    w_end_t = (jnp.exp(a_last - a_cs) * dtv).T

    lane = lax.broadcasted_iota(jnp.int32, (L, LANES), 1)
    lane_row = lax.broadcasted_iota(jnp.int32, (1, LANES), 1)
    n_lt = L // LANES
    tril_blk = (lax.broadcasted_iota(jnp.int32, (LANES, LANES), 0)
                >= lax.broadcasted_iota(jnp.int32, (LANES, LANES), 1))

    def decay_block(cb, a_bc2, src_row, bi, bj):
        if bj > bi:
            return jnp.zeros((LANES, LANES), bf16)
        rs, cs = slice(bi * LANES, (bi + 1) * LANES), slice(bj * LANES, (bj + 1) * LANES)
        seg = a_bc2[rs, :] - src_row[:, cs]
        if bj == bi:
            seg = jnp.where(tril_blk, seg, -jnp.inf)
        return (cb[rs, cs] * jnp.exp2(seg)).astype(bf16)

    for g in range(G):
        b_g = b_scr[:, g * N:(g + 1) * N]
        c_g = c_scr[:, g * N:(g + 1) * N]
        cb = lax.dot_general(c_g, b_g, (((1,), (1,)), ((), ())), preferred_element_type=f32)
        b_g_t = b_g.astype(f32).T
        st_g = st_scr[g]
        y_off = jnp.dot(c_g, st_g.astype(bf16), preferred_element_type=f32)
        for pr in range(HG // 2):
            h0 = g * HG + 2 * pr
            cols = slice(h0 * P, (h0 + 2) * P)
            lcols = slice(2 * pr * P, (2 * pr + 2) * P)
            m_parts, bt_parts, ea_parts = [], [], []
            for h in (h0, h0 + 1):
                a_bc2 = jnp.broadcast_to(a2[:, h:h + 1], (L, LANES))
                src_row = src2_t[h:h + 1, :]
                m_parts.append(jnp.concatenate(
                    [jnp.concatenate([decay_block(cb, a_bc2, src_row, bi, bj) for bj in range(n_lt)], axis=1)
                     for bi in range(n_lt)], axis=0))
                bt_parts.append((b_g_t * w_end_t[h:h + 1, :]).astype(bf16))
                ea_parts.append(jnp.exp2(a_bc2))
            xsb = xsb_scr[:, cols]
            zero = jnp.zeros_like(xsb)
            lane_b = lax.broadcasted_iota(jnp.int32, xsb.shape, 1)
            x_bd = jnp.concatenate([jnp.where(lane_b < P, xsb, zero),
                                    jnp.where(lane_b >= P, xsb, zero)], axis=0)
            y_pair = jnp.dot(jnp.concatenate(m_parts, axis=1), x_bd, preferred_element_type=f32)
            ea = jnp.where(lane < P, ea_parts[0], ea_parts[1])
            y_scr[:, cols] = y_pair + y_off[:, lcols] * ea + xs_scr[:, cols] * dskip_ref[:, cols]
            upd = jnp.dot(jnp.concatenate(bt_parts, axis=1), x_bd, preferred_element_type=f32)
            el = jnp.where(lane_row < P, exp_last[:, h0:h0 + 1], exp_last[:, h0 + 1:h0 + 2])
            st_scr[g, :, lcols] = st_g[:, lcols] * el + upd

    gw = SSM_INNER // G
    for g in range(G):
        gs = slice(g * gw, (g + 1) * gw)
        zz = z_ref[:, gs].astype(f32)
        y = y_scr[:, gs] * _silu(zz)
        ms = jnp.mean(y * y, axis=-1, keepdims=True)
        o_ref[:, gs] = (y * lax.rsqrt(ms + EPS) * normg_ref[:, gs]).astype(o_ref.dtype)


def _ssd(proj3, dt3, conv_w, conv_b, dt_bias, a_log, d_skip_ch, norm_g):
    b, s, _ = proj3.shape
    L = SSM_CHUNK
    assert s % L == 0
    full = lambda shape: pl.BlockSpec(shape, lambda bi, c: (0,) * len(shape))
    return pl.pallas_call(
        _ssd_kernel,
        grid=(b, s // L),
        in_specs=[
            pl.BlockSpec((None, L, SSM_CONV_DIM), lambda bi, c: (bi, c, COL_XBC // SSM_CONV_DIM)),
            pl.BlockSpec((None, L, SSM_INNER), lambda bi, c: (bi, c, COL_Z // SSM_INNER)),
            pl.BlockSpec((None, L, LANES), lambda bi, c: (bi, c, 0)),
            full((SSM_CONV, SSM_CONV_DIM)), full((1, SSM_CONV_DIM)),
            full((1, LANES)), full((1, LANES)), full((1, SSM_INNER)), full((1, SSM_INNER)),
        ],
        out_specs=pl.BlockSpec((None, L, SSM_INNER), lambda bi, c: (bi, c, 0)),
        out_shape=jax.ShapeDtypeStruct((b, s, SSM_INNER), jnp.bfloat16),
        scratch_shapes=[
            pltpu.VMEM((L + SUBLANES, SSM_CONV_DIM), jnp.float32),
            pltpu.VMEM((L + SUBLANES, SSM_CONV_DIM), jnp.float32),
            pltpu.VMEM((L, SSM_INNER), jnp.float32),
            pltpu.VMEM((L, SSM_INNER), jnp.bfloat16),
            pltpu.VMEM((L, SSM_GROUPS * SSM_STATE), jnp.bfloat16),
            pltpu.VMEM((L, SSM_GROUPS * SSM_STATE), jnp.bfloat16),
            pltpu.VMEM((L, SSM_INNER), jnp.float32),
            pltpu.VMEM((SSM_GROUPS, SSM_STATE, SSM_HEADS_PER_GROUP * SSM_HEAD_DIM), jnp.float32),
        ],
        compiler_params=pltpu.CompilerParams(
            dimension_semantics=("arbitrary", "arbitrary"),
            vmem_limit_bytes=VMEM_LIMIT_BYTES),
        name="ssd",
    )(proj3, proj3, dt3, conv_w, conv_b, dt_bias, a_log, d_skip_ch, norm_g)


def _out_kernel(x_ref, oatt_ref, yssm_ref, ga_ref, gs_ref, watt_ref, wssm_ref, wout_ref, gpost_ref, o_ref):
    y_att = jnp.dot(oatt_ref[...], watt_ref[...], preferred_element_type=jnp.float32)
    y_ssm = jnp.dot(yssm_ref[...], wssm_ref[...], preferred_element_type=jnp.float32)
    ga = ga_ref[...].astype(jnp.float32)
    gs = gs_ref[...].astype(jnp.float32)
    mixed = y_att / (1.0 + jnp.exp(-ga)) + y_ssm / (1.0 + jnp.exp(-gs))
    out = jnp.dot(mixed.astype(jnp.bfloat16), wout_ref[...], preferred_element_type=jnp.float32)
    ms = jnp.mean(out * out, axis=-1, keepdims=True)
    o_ref[...] = x_ref[...] + out * lax.rsqrt(ms + EPS) * gpost_ref[...]


def _out_proj(x2, o_att, y_ssm, proj2, w_att, w_ssm, w_out, g_post):
    m = x2.shape[0]
    tm = min(OUT_TM, m)
    const = lambda shape: pl.BlockSpec(shape, lambda i: (0, 0))
    return pl.pallas_call(
        _out_kernel,
        grid=(m // tm,),
        in_specs=[
            pl.BlockSpec((tm, D_MODEL), lambda i: (i, 0)),
            pl.BlockSpec((tm, ATT_WIDTH), lambda i: (i, 0)),
            pl.BlockSpec((tm, SSM_INNER), lambda i: (i, 0)),
            pl.BlockSpec((tm, D_MODEL), lambda i: (i, COL_GATE // D_MODEL)),
            pl.BlockSpec((tm, D_MODEL), lambda i: (i, COL_GATE // D_MODEL + 1)),
            const((ATT_WIDTH, D_MODEL)), const((SSM_INNER, D_MODEL)), const((D_MODEL, D_MODEL)),
            const((1, D_MODEL)),
        ],
        out_specs=pl.BlockSpec((tm, D_MODEL), lambda i: (i, 0)),
        out_shape=jax.ShapeDtypeStruct((m, D_MODEL), jnp.float32),
        compiler_params=pltpu.CompilerParams(
            dimension_semantics=("arbitrary",),
            vmem_limit_bytes=VMEM_LIMIT_BYTES),
        name="out_proj",
    )(x2, o_att, y_ssm, proj2, proj2, w_att, w_ssm, w_out, g_post)


def kernel(x, g_pre, w_in, att_lambda_q1, att_lambda_k1, att_lambda_q2, att_lambda_k2, att_subln_g, rel_bias,
           conv_w, conv_b, dt_bias, a_log, d_skip, ssm_norm_g, w_att_proj, w_ssm_proj, w_out, g_post):
    b, s, d = x.shape
    assert d == D_MODEL and g_pre.shape[0] == 1
    m = b * s
    f32, bf16 = jnp.float32, jnp.bfloat16
    t = min(ATT_BLOCK, s)
    assert s % t == 0 and s % SSM_CHUNK == 0

    w = w_in[0]
    o_q, o_k, o_v, o_g = 0, ATT_WIDTH, 2 * ATT_WIDTH, 3 * ATT_WIDTH
    o_z = 4 * ATT_WIDTH
    o_xbc = o_z + SSM_INNER
    o_dt = o_xbc + SSM_CONV_DIM
    o_gate = o_dt + SSM_HEADS
    w_main = jnp.concatenate([w[:, o_q:o_z], w[:, o_xbc:o_dt], w[:, o_z:o_xbc], w[:, o_gate:]], axis=1).astype(bf16)
    w_dt = jnp.pad(w[:, o_dt:o_gate], ((0, 0), (0, LANES - SSM_HEADS))).astype(bf16)
    col_scale = jnp.concatenate([jnp.full((1, ATT_WIDTH), ATT_QK_DIM ** -0.5 * LOG2E, f32),
                                 jnp.ones((1, PROJ_COLS - ATT_WIDTH), f32)], axis=1)

    x2 = x.reshape(m, d)
    proj, dt_raw = _in_proj(x2, g_pre, w_main, w_dt, col_scale)
    proj3 = proj.reshape(b, s, PROJ_COLS)
    dt3 = dt_raw.reshape(b, s, LANES)

    bias = _bias_tiles(rel_bias, t)
    o_att = _attention(proj3, bias, att_subln_g, att_lambda_q1, att_lambda_k1, att_lambda_q2, att_lambda_k2, t)

    pad_h = lambda v: jnp.pad(v, ((0, 0), (0, LANES - SSM_HEADS)))
    d_skip_ch = jnp.repeat(d_skip, SSM_HEAD_DIM, axis=1)
    y_ssm = _ssd(proj3, dt3, conv_w[0], conv_b, pad_h(dt_bias), pad_h(a_log), d_skip_ch, ssm_norm_g)

    out = _out_proj(x2, o_att.reshape(m, ATT_WIDTH), y_ssm.reshape(m, SSM_INNER), proj,
                    w_att_proj[0].astype(bf16), w_ssm_proj[0].astype(bf16), w_out[0].astype(bf16), g_post)
    return out.reshape(b, s, d)
```

```python
import functools
import math

import numpy as np
import jax
import jax.numpy as jnp
from jax import lax
from jax.experimental import pallas as pl
from jax.experimental.pallas import tpu as pltpu

D_MODEL = 1024
ATT_HEADS = 8
ATT_QK_DIM = 64
ATT_V_DIM = 2 * ATT_QK_DIM
ATT_WIDTH = ATT_HEADS * ATT_V_DIM
REL_BUCKETS = 32
REL_MAX_DIST = 128
SSM_INNER = 2 * D_MODEL
SSM_HEAD_DIM = 64
SSM_HEADS = SSM_INNER // SSM_HEAD_DIM
SSM_GROUPS = 8
SSM_HEADS_PER_GROUP = SSM_HEADS // SSM_GROUPS
SSM_STATE = 128
SSM_CONV = 4
SSM_CHUNK = 256
SSM_CONV_DIM = SSM_INNER + 2 * SSM_GROUPS * SSM_STATE
EPS = 1e-6
LAMBDA_INIT = 0.8 - 0.6 * math.exp(-0.3 * 0)

LANES = 128
SUBLANES = 8
VMEM_LIMIT_BYTES = 56 * 1024 * 1024
LOG2E = math.log2(math.e)
NEG_BIG = -1e30

COL_Q = 0
COL_K = COL_Q + ATT_WIDTH
COL_V = COL_K + ATT_WIDTH
COL_GATT = COL_V + ATT_WIDTH
COL_XBC = COL_GATT + ATT_WIDTH
COL_Z = COL_XBC + SSM_CONV_DIM
COL_GATE = COL_Z + SSM_INNER
PROJ_COLS = COL_GATE + 2 * D_MODEL

ATT_BLOCK = 512
N_BIAS_TILES = 3
ATT_SOFTMAX_ROWS = 64
ATT_UNROLL = 4
PROJ_TM = 1024
PROJ_TN = 1024
OUT_TM = 512


def _t5_bucket_thresholds():
    max_exact = REL_BUCKETS // 2
    n = np.arange(0, 4 * REL_MAX_DIST)
    nf = np.maximum(n, 1).astype(np.float32)
    large = max_exact + (np.log(nf / np.float32(max_exact)) / np.float32(math.log(REL_MAX_DIST / max_exact))
                         * np.float32(REL_BUCKETS - max_exact)).astype(np.int32)
    large = np.minimum(large, REL_BUCKETS - 1)
    bucket = np.where(n < max_exact, n, large)
    assert np.all(np.diff(bucket) >= 0)
    thr = [int(np.argmax(bucket >= b)) for b in range(REL_BUCKETS)]
    assert bucket[thr[-1]] == REL_BUCKETS - 1
    return thr


_BUCKET_THR = _t5_bucket_thresholds()


def _in_proj_kernel(x_ref, g_ref, w_ref, wdt_ref, scale_ref, o_ref, dt_ref, h_scr):
    @pl.when(pl.program_id(1) == 0)
    def _():
        xf = x_ref[...]
        ms = jnp.mean(xf * xf, axis=-1, keepdims=True)
        h = (xf * lax.rsqrt(ms + EPS) * g_ref[...]).astype(jnp.bfloat16)
        h_scr[...] = h
        dt_ref[...] = jnp.dot(h, wdt_ref[...], preferred_element_type=jnp.float32)

    acc = jnp.dot(h_scr[...], w_ref[...], preferred_element_type=jnp.float32)
    o_ref[...] = (acc * scale_ref[...]).astype(o_ref.dtype)


def _in_proj(x2, g_pre, w_main, w_dt, col_scale):
    m = x2.shape[0]
    tm = min(PROJ_TM, m)
    tn = PROJ_TN
    return pl.pallas_call(
        _in_proj_kernel,
        grid=(m // tm, PROJ_COLS // tn),
        in_specs=[
            pl.BlockSpec((tm, D_MODEL), lambda i, j: (i, 0)),
            pl.BlockSpec((1, D_MODEL), lambda i, j: (0, 0)),
            pl.BlockSpec((D_MODEL, tn), lambda i, j: (0, j)),
            pl.BlockSpec((D_MODEL, LANES), lambda i, j: (0, 0)),
            pl.BlockSpec((1, tn), lambda i, j: (0, j)),
        ],
        out_specs=[
            pl.BlockSpec((tm, tn), lambda i, j: (i, j)),
            pl.BlockSpec((tm, LANES), lambda i, j: (i, 0)),
        ],
        out_shape=[
            jax.ShapeDtypeStruct((m, PROJ_COLS), jnp.bfloat16),
            jax.ShapeDtypeStruct((m, LANES), jnp.float32),
        ],
        scratch_shapes=[pltpu.VMEM((tm, D_MODEL), jnp.bfloat16)],
        compiler_params=pltpu.CompilerParams(
            dimension_semantics=("arbitrary", "arbitrary"),
            vmem_limit_bytes=VMEM_LIMIT_BYTES),
        name="in_proj",
    )(x2, g_pre, w_main, w_dt, col_scale)


def _bias_kernel(rb_ref, o_ref, *, t):
    h = pl.program_id(0)
    row = lax.broadcasted_iota(jnp.int32, (t, t), 0)
    col = lax.broadcasted_iota(jnp.int32, (t, t), 1)
    far = rb_ref[REL_BUCKETS - 1, h]
    for d in range(N_BIAS_TILES):
        dist = d * t + row - col
        val = jnp.full((t, t), rb_ref[0, h], jnp.float32)
        for b in range(1, REL_BUCKETS):
            val = jnp.where(dist >= _BUCKET_THR[b], rb_ref[b, h], val)
        val = (val - far) * LOG2E
        if d == 0:
            val = jnp.where(dist >= 0, val, NEG_BIG)
        o_ref[d] = val


def _bias_tiles(rel_bias, t):
    assert t + 1 >= _BUCKET_THR[-1]
    return pl.pallas_call(
        functools.partial(_bias_kernel, t=t),
        grid=(ATT_HEADS,),
        in_specs=[pl.BlockSpec(memory_space=pltpu.SMEM)],
        out_specs=pl.BlockSpec((None, N_BIAS_TILES, t, t), lambda h: (h, 0, 0, 0)),
        out_shape=jax.ShapeDtypeStruct((ATT_HEADS, N_BIAS_TILES, t, t), jnp.float32),
        compiler_params=pltpu.CompilerParams(dimension_semantics=("arbitrary",)),
        name="t5_bias",
    )(rel_bias)


def _attn_kernel(q_ref, k_ref, v_ref, gatt_ref, bias_ref, subln_ref, lq1_ref, lk1_ref, lq2_ref, lk2_ref,
                 o_ref, vaug_scr, m_scr, acc_scr, s0_scr, s1_scr, p0_scr, p1_scr, al0_scr, al1_scr, *, t, nq):
    dv = ATT_V_DIM
    s_bufs, p_bufs, al_bufs = (s0_scr, s1_scr), (p0_scr, p1_scr), (al0_scr, al1_scr)
    steps = [(qi, j) for qi in range(nq) for j in range(qi + 1)]
    n_steps = len(steps)

    vaug_scr[:, :dv] = v_ref[...]
    vaug_scr[:, dv:] = jnp.ones((v_ref.shape[0], dv), v_ref.dtype)
    acc_scr[...] = jnp.zeros(acc_scr.shape, jnp.float32)
    m_scr[...] = jnp.full(m_scr.shape, -jnp.inf, jnp.float32)

    def rows(idx):
        if isinstance(idx, int):
            return pl.ds(idx * t, t)
        return pl.ds(pl.multiple_of(idx * t, t), t)

    def stage_a(qi, j, s_dst):
        q = q_ref[rows(qi), :]
        lane = lax.broadcasted_iota(jnp.int32, q.shape, 1)
        zero = jnp.zeros_like(q)
        qs = jnp.concatenate([jnp.where(lane < ATT_QK_DIM, q, zero),
                              jnp.where(lane >= ATT_QK_DIM, q, zero)], axis=0)
        d = min(qi - j, N_BIAS_TILES - 1) if isinstance(qi, int) else jnp.minimum(qi - j, N_BIAS_TILES - 1)
        s = lax.dot_general(qs, k_ref[rows(j), :], (((1,), (1,)), ((), ())),
                            preferred_element_type=jnp.float32)
        s_dst[...] = (s.reshape(2, t, t) + bias_ref[d][None]).reshape(2 * t, t)

    def stage_b(qi, j, s_src, p_dst, al_dst):
        for r0 in range(0, 2 * t, ATT_SOFTMAX_ROWS):
            rs = slice(r0, r0 + ATT_SOFTMAX_ROWS)
            m_old = m_scr[rs, :]
            if isinstance(j, int):
                m_prev = jnp.full(m_old.shape, -jnp.inf, jnp.float32) if j == 0 else m_old
            else:
                m_prev = jnp.where(j == 0, -jnp.inf, m_old)
            m_new = jnp.maximum(m_prev, jnp.max(s_src[rs, :], axis=-1, keepdims=True))
            al_dst[rs, :] = jnp.exp2(m_prev - m_new)
            m_scr[rs, :] = m_new
        for r0 in range(0, 2 * t, ATT_SOFTMAX_ROWS):
            rs = slice(r0, r0 + ATT_SOFTMAX_ROWS)
            p_dst[rs, :] = jnp.exp2(s_src[rs, :] - m_scr[rs, :][:, :1]).astype(jnp.bfloat16)

    def finalize(qi):
        acc = acc_scr[qi % 2] if isinstance(qi, int) else acc_scr[qi & 1]
        o = acc[:, :dv] / acc[:, dv:]
        lam = (jnp.exp(jnp.sum(lq1_ref[...] * lk1_ref[...], axis=-1, keepdims=True))
               - jnp.exp(jnp.sum(lq2_ref[...] * lk2_ref[...], axis=-1, keepdims=True))
               + LAMBDA_INIT)
        o = o[:t] - lam * o[t:]
        ms = jnp.mean(o * o, axis=-1, keepdims=True)
        o = o * lax.rsqrt(ms + EPS) * subln_ref[...] * (1.0 - LAMBDA_INIT)
        g = gatt_ref[rows(qi), :].astype(jnp.float32)
        o = o * (g / (1.0 + jnp.exp(-g)))
        o_ref[rows(qi), :] = o.astype(o_ref.dtype)

    def stage_c(qi, j, p_src, al_src):
        slot = qi % 2 if isinstance(qi, int) else qi & 1
        pv = jnp.dot(p_src[...], vaug_scr[rows(j), :], preferred_element_type=jnp.float32)
        al = al_src[...]
        acc_scr[slot] = acc_scr[slot] * jnp.concatenate([al, al], axis=1) + pv

    def iteration(par, a, b, c):
        if a is not None:
            stage_a(a[0], a[1], s_bufs[par])
        if b is not None:
            stage_b(b[0], b[1], s_bufs[1 - par], p_bufs[1 - par], al_bufs[1 - par])
        if c is not None:
            stage_c(c[0], c[1], p_bufs[par], al_bufs[par])

    def static_iteration(it):
        pick = lambda i: steps[i] if 0 <= i < n_steps else None
        iteration(it % 2, pick(it), pick(it - 1), pick(it - 2))
        c = pick(it - 2)
        if c is not None and c[0] == c[1]:
            finalize(c[0])

    def advance(qi, j):
        wrap = j == qi
        return jnp.where(wrap, qi + 1, qi), jnp.where(wrap, 0, j + 1)

    unroll = ATT_UNROLL
    assert unroll % 2 == 0
    lo = 2 + unroll * (unroll - 1) // 2
    lo += lo % 2
    n_trips = max(n_steps - lo, 0) // unroll
    loop_end = lo + unroll * n_trips if n_trips > 0 else 0
    if n_trips > 0:
        for it in range(lo):
            static_iteration(it)

        def trip(_, carry):
            a, b, c = carry[0:2], carry[2:4], carry[4:6]
            done, done_q = jnp.bool_(False), jnp.int32(0)
            for k in range(unroll):
                iteration(k % 2, a, b, c)
                last = c[0] == c[1]
                done, done_q = done | last, jnp.where(last, c[0], done_q)
                a, b, c = advance(*a), a, b
            pl.when(done)(lambda: finalize(done_q))
            return (*a, *b, *c)

        init = tuple(jnp.int32(v) for i in (lo, lo - 1, lo - 2) for v in steps[i])
        lax.fori_loop(0, n_trips, trip, init)
    for it in range(loop_end, n_steps + 2):
        static_iteration(it)


def _attention(proj3, bias, subln_g, lq1, lk1, lq2, lk2, t):
    b, s, _ = proj3.shape
    hd = ATT_V_DIM
    kq, kk, kv, kg = COL_Q // hd, COL_K // hd, COL_V // hd, COL_GATT // hd
    vec = lambda n: pl.BlockSpec((1, n), lambda bi, h: (0, 0))
    head_cols = lambda first: pl.BlockSpec((None, s, hd), lambda bi, h: (bi, 0, first + h))
    return pl.pallas_call(
        functools.partial(_attn_kernel, t=t, nq=s // t),
        grid=(b, ATT_HEADS),
        in_specs=[
            head_cols(kq), head_cols(kk), head_cols(kv), head_cols(kg),
            pl.BlockSpec((None, N_BIAS_TILES, t, t), lambda bi, h: (h, 0, 0, 0)),
            vec(hd), vec(ATT_QK_DIM), vec(ATT_QK_DIM), vec(ATT_QK_DIM), vec(ATT_QK_DIM),
        ],
        out_specs=head_cols(0),
        out_shape=jax.ShapeDtypeStruct((b, s, ATT_WIDTH), jnp.bfloat16),
        scratch_shapes=[
            pltpu.VMEM((s, 2 * hd), jnp.bfloat16),
            pltpu.VMEM((2 * t, LANES), jnp.float32),
            pltpu.VMEM((2, 2 * t, 2 * hd), jnp.float32),
            pltpu.VMEM((2 * t, t), jnp.float32), pltpu.VMEM((2 * t, t), jnp.float32),
            pltpu.VMEM((2 * t, t), jnp.bfloat16), pltpu.VMEM((2 * t, t), jnp.bfloat16),
            pltpu.VMEM((2 * t, LANES), jnp.float32), pltpu.VMEM((2 * t, LANES), jnp.float32),
        ],
        compiler_params=pltpu.CompilerParams(
            dimension_semantics=("arbitrary", "arbitrary"),
            vmem_limit_bytes=VMEM_LIMIT_BYTES),
        name="diff_attn",
    )(proj3, proj3, proj3, proj3, bias, subln_g, lq1, lk1, lq2, lk2)


def _silu(x):
    h = 0.5 * x
    return h + h * jnp.tanh(h)


def _ssd_kernel(xbc_ref, z_ref, dt_ref, shift_ref, convw_ref, convb_ref, dtb_ref, alog_ref, dskip_ref, normg_ref,
                o_ref, tail_scr, xs_scr, xsb_scr, b_scr, c_scr, y_scr, st_scr):
    c = pl.program_id(1)
    L = SSM_CHUNK
    P, N, G, HG = SSM_HEAD_DIM, SSM_STATE, SSM_GROUPS, SSM_HEADS_PER_GROUP
    halo = SUBLANES
    f32, bf16 = jnp.float32, jnp.bfloat16

    @pl.when(c == 0)
    def _():
        tail_scr[...] = jnp.zeros(tail_scr.shape, f32)
        st_scr[...] = jnp.zeros(st_scr.shape, f32)

    ct = 512
    for t0 in range(0, SSM_CONV_DIM, ct):
        cs = slice(t0, t0 + ct)
        w = [convw_ref[k:k + 1, cs] for k in range(SSM_CONV)]
        u = xbc_ref[:, cs].astype(f32)
        taps = jnp.concatenate([(u * w[SSM_CONV - 1 - k]).astype(bf16) for k in range(1, SSM_CONV)], axis=0)
        acc = (convb_ref[:, cs] + w[SSM_CONV - 1] * u
               + jnp.dot(shift_ref[...], taps, preferred_element_type=f32))
        corr = sum(w[SSM_CONV - 1 - k] * tail_scr[halo - k:2 * halo - k, cs] for k in range(1, SSM_CONV))
        act = _silu(jnp.concatenate([acc[:halo] + corr, acc[halo:]], axis=0))
        tail_scr[:halo, cs] = u[L - halo:]
        if t0 < SSM_INNER:
            xs_scr[:, cs] = act
            xsb_scr[:, cs] = act.astype(bf16)
        elif t0 < SSM_INNER + G * N:
            b_scr[:, t0 - SSM_INNER:t0 - SSM_INNER + ct] = act.astype(bf16)
        else:
            c_scr[:, t0 - SSM_INNER - G * N:t0 - SSM_INNER - G * N + ct] = act.astype(bf16)

    xdt = dt_ref[...] + dtb_ref[...]
    dtv = jnp.maximum(xdt, 0.0) + jnp.log1p(jnp.exp(-jnp.abs(xdt)))
    adt = dtv * (-jnp.exp(alog_ref[...]))
    row = lax.broadcasted_iota(jnp.int32, (L, L), 0)
    col = lax.broadcasted_iota(jnp.int32, (L, L), 1)
    tril = row >= col
    tri = jnp.where(tril, 1.0, 0.0).astype(bf16)
    hi = adt.astype(bf16)
    r1 = adt - hi.astype(f32)
    mid = r1.astype(bf16)
    lo = (r1 - mid.astype(f32)).astype(bf16)
    a_cs = (jnp.dot(tri, hi, preferred_element_type=f32)
            + jnp.dot(tri, mid, preferred_element_type=f32)
            + jnp.dot(tri, lo, preferred_element_type=f32))
    a_last = a_cs[L - 1:L, :]
    exp_last = jnp.exp(a_last)
    a2 = a_cs * LOG2E
    src2_t = (a2 - jnp.log2(dtv)).T
    w_end_t = (jnp.exp(a_last - a_cs) * dtv).T

    lane = lax.broadcasted_iota(jnp.int32, (L, LANES), 1)
    lane_row = lax.broadcasted_iota(jnp.int32, (1, LANES), 1)
    n_lt = L // LANES
    tril_blk = (lax.broadcasted_iota(jnp.int32, (LANES, LANES), 0)
                >= lax.broadcasted_iota(jnp.int32, (LANES, LANES), 1))

    def decay_block(cb, a_bc2, src_row, bi, bj):
        if bj > bi:
            return jnp.zeros((LANES, LANES), bf16)
        rs, cs = slice(bi * LANES, (bi + 1) * LANES), slice(bj * LANES, (bj + 1) * LANES)
        seg = a_bc2[rs, :] - src_row[:, cs]
        if bj == bi:
            seg = jnp.where(tril_blk, seg, -jnp.inf)
        return (cb[rs, cs] * jnp.exp2(seg)).astype(bf16)

    for g in range(G):
        b_g = b_scr[:, g * N:(g + 1) * N]
        c_g = c_scr[:, g * N:(g + 1) * N]
        cb = lax.dot_general(c_g, b_g, (((1,), (1,)), ((), ())), preferred_element_type=f32)
        b_g_t = b_g.astype(f32).T
        st_g = st_scr[g]
        y_off = jnp.dot(c_g, st_g.astype(bf16), preferred_element_type=f32)
        for pr in range(HG // 2):
            h0 = g * HG + 2 * pr
            cols = slice(h0 * P, (h0 + 2) * P)
            lcols = slice(2 * pr * P, (2 * pr + 2) * P)
            m_parts, bt_parts, ea_parts = [], [], []
            for h in (h0, h0 + 1):
                a_bc2 = jnp.broadcast_to(a2[:, h:h + 1], (L, LANES))
                src_row = src2_t[h:h + 1, :]
                m_parts.append(jnp.concatenate(
                    [jnp.concatenate([decay_block(cb, a_bc2, src_row, bi, bj) for bj in range(n_lt)], axis=1)
                     for bi in range(n_lt)], axis=0))
                bt_parts.append((b_g_t * w_end_t[h:h + 1, :]).astype(bf16))
                ea_parts.append(jnp.exp2(a_bc2))
            xsb = xsb_scr[:, cols]
            zero = jnp.zeros_like(xsb)
            lane_b = lax.broadcasted_iota(jnp.int32, xsb.shape, 1)
            x_bd = jnp.concatenate([jnp.where(lane_b < P, xsb, zero),
                                    jnp.where(lane_b >= P, xsb, zero)], axis=0)
            y_pair = jnp.dot(jnp.concatenate(m_parts, axis=1), x_bd, preferred_element_type=f32)
            ea = jnp.where(lane < P, ea_parts[0], ea_parts[1])
            y_scr[:, cols] = y_pair + y_off[:, lcols] * ea + xs_scr[:, cols] * dskip_ref[:, cols]
            upd = jnp.dot(jnp.concatenate(bt_parts, axis=1), x_bd, preferred_element_type=f32)
            el = jnp.where(lane_row < P, exp_last[:, h0:h0 + 1], exp_last[:, h0 + 1:h0 + 2])
            st_scr[g, :, lcols] = st_g[:, lcols] * el + upd

    gw = SSM_INNER // G
    for g in range(G):
        gs = slice(g * gw, (g + 1) * gw)
        zz = z_ref[:, gs].astype(f32)
        y = y_scr[:, gs] * _silu(zz)
        ms = jnp.mean(y * y, axis=-1, keepdims=True)
        o_ref[:, gs] = (y * lax.rsqrt(ms + EPS) * normg_ref[:, gs]).astype(o_ref.dtype)


def _ssd(proj3, dt3, conv_w, conv_b, dt_bias, a_log, d_skip_ch, norm_g):
    b, s, _ = proj3.shape
    L = SSM_CHUNK
    assert s % L == 0
    full = lambda shape: pl.BlockSpec(shape, lambda bi, c: (0,) * len(shape))
    t_idx = np.arange(L)
    shift = np.concatenate([(t_idx[None, :] == t_idx[:, None] - k) for k in range(1, SSM_CONV)], axis=1)
    shift = jnp.asarray(shift, jnp.bfloat16)
    return pl.pallas_call(
        _ssd_kernel,
        grid=(b, s // L),
        in_specs=[
            pl.BlockSpec((None, L, SSM_CONV_DIM), lambda bi, c: (bi, c, COL_XBC // SSM_CONV_DIM)),
            pl.BlockSpec((None, L, SSM_INNER), lambda bi, c: (bi, c, COL_Z // SSM_INNER)),
            pl.BlockSpec((None, L, LANES), lambda bi, c: (bi, c, 0)),
            full((L, (SSM_CONV - 1) * L)),
            full((SSM_CONV, SSM_CONV_DIM)), full((1, SSM_CONV_DIM)),
            full((1, LANES)), full((1, LANES)), full((1, SSM_INNER)), full((1, SSM_INNER)),
        ],
        out_specs=pl.BlockSpec((None, L, SSM_INNER), lambda bi, c: (bi, c, 0)),
        out_shape=jax.ShapeDtypeStruct((b, s, SSM_INNER), jnp.bfloat16),
        scratch_shapes=[
            pltpu.VMEM((2 * SUBLANES, SSM_CONV_DIM), jnp.float32),
            pltpu.VMEM((L, SSM_INNER), jnp.float32),
            pltpu.VMEM((L, SSM_INNER), jnp.bfloat16),
            pltpu.VMEM((L, SSM_GROUPS * SSM_STATE), jnp.bfloat16),
            pltpu.VMEM((L, SSM_GROUPS * SSM_STATE), jnp.bfloat16),
            pltpu.VMEM((L, SSM_INNER), jnp.float32),
            pltpu.VMEM((SSM_GROUPS, SSM_STATE, SSM_HEADS_PER_GROUP * SSM_HEAD_DIM), jnp.float32),
        ],
        compiler_params=pltpu.CompilerParams(
            dimension_semantics=("arbitrary", "arbitrary"),
            vmem_limit_bytes=VMEM_LIMIT_BYTES),
        name="ssd",
    )(proj3, proj3, dt3, shift, conv_w, conv_b, dt_bias, a_log, d_skip_ch, norm_g)


def _out_kernel(x_ref, oatt_ref, yssm_ref, ga_ref, gs_ref, watt_ref, wssm_ref, wout_ref, gpost_ref, o_ref):
    y_att = jnp.dot(oatt_ref[...], watt_ref[...], preferred_element_type=jnp.float32)
    y_ssm = jnp.dot(yssm_ref[...], wssm_ref[...], preferred_element_type=jnp.float32)
    ga = ga_ref[...].astype(jnp.float32)
    gs = gs_ref[...].astype(jnp.float32)
    mixed = y_att / (1.0 + jnp.exp(-ga)) + y_ssm / (1.0 + jnp.exp(-gs))
    out = jnp.dot(mixed.astype(jnp.bfloat16), wout_ref[...], preferred_element_type=jnp.float32)
    ms = jnp.mean(out * out, axis=-1, keepdims=True)
    o_ref[...] = x_ref[...] + out * lax.rsqrt(ms + EPS) * gpost_ref[...]


def _out_proj(x2, o_att, y_ssm, proj2, w_att, w_ssm, w_out, g_post):
    m = x2.shape[0]
    tm = min(OUT_TM, m)
    const = lambda shape: pl.BlockSpec(shape, lambda i: (0, 0))
    return pl.pallas_call(
        _out_kernel,
        grid=(m // tm,),
        in_specs=[
            pl.BlockSpec((tm, D_MODEL), lambda i: (i, 0)),
            pl.BlockSpec((tm, ATT_WIDTH), lambda i: (i, 0)),
            pl.BlockSpec((tm, SSM_INNER), lambda i: (i, 0)),
            pl.BlockSpec((tm, D_MODEL), lambda i: (i, COL_GATE // D_MODEL)),
            pl.BlockSpec((tm, D_MODEL), lambda i: (i, COL_GATE // D_MODEL + 1)),
            const((ATT_WIDTH, D_MODEL)), const((SSM_INNER, D_MODEL)), const((D_MODEL, D_MODEL)),
            const((1, D_MODEL)),
        ],
        out_specs=pl.BlockSpec((tm, D_MODEL), lambda i: (i, 0)),
        out_shape=jax.ShapeDtypeStruct((m, D_MODEL), jnp.float32),
        compiler_params=pltpu.CompilerParams(
            dimension_semantics=("arbitrary",),
            vmem_limit_bytes=VMEM_LIMIT_BYTES),
        name="out_proj",
    )(x2, o_att, y_ssm, proj2, proj2, w_att, w_ssm, w_out, g_post)


def kernel(x, g_pre, w_in, att_lambda_q1, att_lambda_k1, att_lambda_q2, att_lambda_k2, att_subln_g, rel_bias,
           conv_w, conv_b, dt_bias, a_log, d_skip, ssm_norm_g, w_att_proj, w_ssm_proj, w_out, g_post):
    b, s, d = x.shape
    assert d == D_MODEL and g_pre.shape[0] == 1
    m = b * s
    f32, bf16 = jnp.float32, jnp.bfloat16
    t = min(ATT_BLOCK, s)
    assert s % t == 0 and s % SSM_CHUNK == 0

    w = w_in[0]
    o_q, o_k, o_v, o_g = 0, ATT_WIDTH, 2 * ATT_WIDTH, 3 * ATT_WIDTH
    o_z = 4 * ATT_WIDTH
    o_xbc = o_z + SSM_INNER
    o_dt = o_xbc + SSM_CONV_DIM
    o_gate = o_dt + SSM_HEADS
    w_main = jnp.concatenate([w[:, o_q:o_z], w[:, o_xbc:o_dt], w[:, o_z:o_xbc], w[:, o_gate:]], axis=1).astype(bf16)
    w_dt = jnp.pad(w[:, o_dt:o_gate], ((0, 0), (0, LANES - SSM_HEADS))).astype(bf16)
    col_scale = jnp.concatenate([jnp.full((1, ATT_WIDTH), ATT_QK_DIM ** -0.5 * LOG2E, f32),
                                 jnp.ones((1, PROJ_COLS - ATT_WIDTH), f32)], axis=1)

    x2 = x.reshape(m, d)
    proj, dt_raw = _in_proj(x2, g_pre, w_main, w_dt, col_scale)
    proj3 = proj.reshape(b, s, PROJ_COLS)
    dt3 = dt_raw.reshape(b, s, LANES)

    bias = _bias_tiles(rel_bias, t)
    o_att = _attention(proj3, bias, att_subln_g, att_lambda_q1, att_lambda_k1, att_lambda_q2, att_lambda_k2, t)

    pad_h = lambda v: jnp.pad(v, ((0, 0), (0, LANES - SSM_HEADS)))
    d_skip_ch = jnp.repeat(d_skip, SSM_HEAD_DIM, axis=1)
    y_ssm = _ssd(proj3, dt3, conv_w[0], conv_b, pad_h(dt_bias), pad_h(a_log), d_skip_ch, ssm_norm_g)

    out = _out_proj(x2, o_att.reshape(m, ATT_WIDTH), y_ssm.reshape(m, SSM_INNER), proj,
                    w_att_proj[0].astype(bf16), w_ssm_proj[0].astype(bf16), w_out[0].astype(bf16), g_post)
    return out.reshape(b, s, d)
```

```python
import functools
import math

import numpy as np
import jax
import jax.numpy as jnp
from jax import lax
from jax.experimental import pallas as pl
from jax.experimental.pallas import tpu as pltpu

D_MODEL = 1024
ATT_HEADS = 8
ATT_QK_DIM = 64
ATT_V_DIM = 2 * ATT_QK_DIM
ATT_WIDTH = ATT_HEADS * ATT_V_DIM
REL_BUCKETS = 32
REL_MAX_DIST = 128
SSM_INNER = 2 * D_MODEL
SSM_HEAD_DIM = 64
SSM_HEADS = SSM_INNER // SSM_HEAD_DIM
SSM_GROUPS = 8
SSM_HEADS_PER_GROUP = SSM_HEADS // SSM_GROUPS
SSM_STATE = 128
SSM_CONV = 4
SSM_CHUNK = 256
SSM_CONV_DIM = SSM_INNER + 2 * SSM_GROUPS * SSM_STATE
EPS = 1e-6
LAMBDA_INIT = 0.8 - 0.6 * math.exp(-0.3 * 0)

LANES = 128
SUBLANES = 8
VMEM_LIMIT_BYTES = 56 * 1024 * 1024
LOG2E = math.log2(math.e)
NEG_BIG = -1e30

COL_Q = 0
COL_K = COL_Q + ATT_WIDTH
COL_V = COL_K + ATT_WIDTH
COL_GATT = COL_V + ATT_WIDTH
COL_XBC = COL_GATT + ATT_WIDTH
COL_Z = COL_XBC + SSM_CONV_DIM
COL_GATE = COL_Z + SSM_INNER
PROJ_COLS = COL_GATE + 2 * D_MODEL

ATT_BLOCK = 512
N_BIAS_TILES = 3
ATT_SOFTMAX_ROWS = 64
ATT_UNROLL = 8
ATT_ACC_SLOTS = 4
PROJ_TM = 1024
PROJ_TN = 2048
OUT_TM = 512


def _t5_bucket_thresholds():
    max_exact = REL_BUCKETS // 2
    n = np.arange(0, 4 * REL_MAX_DIST)
    nf = np.maximum(n, 1).astype(np.float32)
    large = max_exact + (np.log(nf / np.float32(max_exact)) / np.float32(math.log(REL_MAX_DIST / max_exact))
                         * np.float32(REL_BUCKETS - max_exact)).astype(np.int32)
    large = np.minimum(large, REL_BUCKETS - 1)
    bucket = np.where(n < max_exact, n, large)
    assert np.all(np.diff(bucket) >= 0)
    thr = [int(np.argmax(bucket >= b)) for b in range(REL_BUCKETS)]
    assert bucket[thr[-1]] == REL_BUCKETS - 1
    return thr


_BUCKET_THR = _t5_bucket_thresholds()


def _in_proj_kernel(x_ref, g_ref, w_ref, wdt_ref, scale_ref, o_ref, dt_ref, h_scr):
    @pl.when(pl.program_id(1) == 0)
    def _():
        xf = x_ref[...]
        ms = jnp.mean(xf * xf, axis=-1, keepdims=True)
        h = (xf * lax.rsqrt(ms + EPS) * g_ref[...]).astype(jnp.bfloat16)
        h_scr[...] = h
        dt_ref[...] = jnp.dot(h, wdt_ref[...], preferred_element_type=jnp.float32)

    acc = jnp.dot(h_scr[...], w_ref[...], preferred_element_type=jnp.float32)
    o_ref[...] = (acc * scale_ref[...]).astype(o_ref.dtype)


def _in_proj(x2, g_pre, w_main, w_dt, col_scale):
    m = x2.shape[0]
    tm = min(PROJ_TM, m)
    tn = PROJ_TN
    return pl.pallas_call(
        _in_proj_kernel,
        grid=(m // tm, PROJ_COLS // tn),
        in_specs=[
            pl.BlockSpec((tm, D_MODEL), lambda i, j: (i, 0)),
            pl.BlockSpec((1, D_MODEL), lambda i, j: (0, 0)),
            pl.BlockSpec((D_MODEL, tn), lambda i, j: (0, j)),
            pl.BlockSpec((D_MODEL, LANES), lambda i, j: (0, 0)),
            pl.BlockSpec((1, tn), lambda i, j: (0, j)),
        ],
        out_specs=[
            pl.BlockSpec((tm, tn), lambda i, j: (i, j)),
            pl.BlockSpec((tm, LANES), lambda i, j: (i, 0)),
        ],
        out_shape=[
            jax.ShapeDtypeStruct((m, PROJ_COLS), jnp.bfloat16),
            jax.ShapeDtypeStruct((m, LANES), jnp.float32),
        ],
        scratch_shapes=[pltpu.VMEM((tm, D_MODEL), jnp.bfloat16)],
        compiler_params=pltpu.CompilerParams(
            dimension_semantics=("arbitrary", "arbitrary"),
            vmem_limit_bytes=VMEM_LIMIT_BYTES),
        name="in_proj",
    )(x2, g_pre, w_main, w_dt, col_scale)


def _bias_kernel(rb_ref, o_ref, *, t):
    h = pl.program_id(0)
    row = lax.broadcasted_iota(jnp.int32, (t, t), 0)
    col = lax.broadcasted_iota(jnp.int32, (t, t), 1)
    far = rb_ref[REL_BUCKETS - 1, h]
    for d in range(N_BIAS_TILES):
        dist = d * t + row - col
        val = jnp.full((t, t), rb_ref[0, h], jnp.float32)
        for b in range(1, REL_BUCKETS):
            val = jnp.where(dist >= _BUCKET_THR[b], rb_ref[b, h], val)
        val = (val - far) * LOG2E
        if d == 0:
            val = jnp.where(dist >= 0, val, NEG_BIG)
        o_ref[d] = val


def _bias_tiles(rel_bias, t):
    assert t + 1 >= _BUCKET_THR[-1]
    return pl.pallas_call(
        functools.partial(_bias_kernel, t=t),
        grid=(ATT_HEADS,),
        in_specs=[pl.BlockSpec(memory_space=pltpu.SMEM)],
        out_specs=pl.BlockSpec((None, N_BIAS_TILES, t, t), lambda h: (h, 0, 0, 0)),
        out_shape=jax.ShapeDtypeStruct((ATT_HEADS, N_BIAS_TILES, t, t), jnp.float32),
        compiler_params=pltpu.CompilerParams(dimension_semantics=("arbitrary",)),
        name="t5_bias",
    )(rel_bias)


def _attn_kernel(q_ref, k_ref, v_ref, gatt_ref, bias_ref, subln_ref, lq1_ref, lk1_ref, lq2_ref, lk2_ref,
                 o_ref, vaug_scr, m_scr, acc_scr, s0_scr, s1_scr, p0_scr, p1_scr, al0_scr, al1_scr, *, t, nq):
    dv = ATT_V_DIM
    s_bufs, p_bufs, al_bufs = (s0_scr, s1_scr), (p0_scr, p1_scr), (al0_scr, al1_scr)
    steps = [(qi, j) for qi in range(nq) for j in range(qi + 1)]
    n_steps = len(steps)

    vaug_scr[:, :dv] = v_ref[...]
    vaug_scr[:, dv:] = jnp.ones((v_ref.shape[0], dv), v_ref.dtype)
    acc_scr[...] = jnp.zeros(acc_scr.shape, jnp.float32)
    m_scr[...] = jnp.full(m_scr.shape, -jnp.inf, jnp.float32)

    def rows(idx):
        if isinstance(idx, int):
            return pl.ds(idx * t, t)
        return pl.ds(pl.multiple_of(idx * t, t), t)

    def stage_a(qi, j, s_dst):
        q = q_ref[rows(qi), :]
        lane = lax.broadcasted_iota(jnp.int32, q.shape, 1)
        zero = jnp.zeros_like(q)
        qs = jnp.concatenate([jnp.where(lane < ATT_QK_DIM, q, zero),
                              jnp.where(lane >= ATT_QK_DIM, q, zero)], axis=0)
        d = min(qi - j, N_BIAS_TILES - 1) if isinstance(qi, int) else jnp.minimum(qi - j, N_BIAS_TILES - 1)
        s = lax.dot_general(qs, k_ref[rows(j), :], (((1,), (1,)), ((), ())),
                            preferred_element_type=jnp.float32)
        s_dst[...] = (s.reshape(2, t, t) + bias_ref[d][None]).reshape(2 * t, t)

    def stage_b(qi, j, s_src, p_dst, al_dst):
        for r0 in range(0, 2 * t, ATT_SOFTMAX_ROWS):
            rs = slice(r0, r0 + ATT_SOFTMAX_ROWS)
            m_old = m_scr[rs, :]
            if isinstance(j, int):
                m_prev = jnp.full(m_old.shape, -jnp.inf, jnp.float32) if j == 0 else m_old
            else:
                m_prev = jnp.where(j == 0, -jnp.inf, m_old)
            m_new = jnp.maximum(m_prev, jnp.max(s_src[rs, :], axis=-1, keepdims=True))
            al_dst[rs, :] = jnp.exp2(m_prev - m_new)
            m_scr[rs, :] = m_new
        for r0 in range(0, 2 * t, ATT_SOFTMAX_ROWS):
            rs = slice(r0, r0 + ATT_SOFTMAX_ROWS)
            p_dst[rs, :] = jnp.exp2(s_src[rs, :] - m_scr[rs, :][:, :1]).astype(jnp.bfloat16)

    def finalize(qi):
        acc = acc_scr[qi % ATT_ACC_SLOTS] if isinstance(qi, int) else acc_scr[qi & (ATT_ACC_SLOTS - 1)]
        o = acc[:, :dv] / acc[:, dv:]
        lam = (jnp.exp(jnp.sum(lq1_ref[...] * lk1_ref[...], axis=-1, keepdims=True))
               - jnp.exp(jnp.sum(lq2_ref[...] * lk2_ref[...], axis=-1, keepdims=True))
               + LAMBDA_INIT)
        o = o[:t] - lam * o[t:]
        ms = jnp.mean(o * o, axis=-1, keepdims=True)
        o = o * lax.rsqrt(ms + EPS) * subln_ref[...] * (1.0 - LAMBDA_INIT)
        g = gatt_ref[rows(qi), :].astype(jnp.float32)
        o = o * (g / (1.0 + jnp.exp(-g)))
        o_ref[rows(qi), :] = o.astype(o_ref.dtype)

    def stage_c(qi, j, p_src, al_src):
        slot = qi % ATT_ACC_SLOTS if isinstance(qi, int) else qi & (ATT_ACC_SLOTS - 1)
        pv = jnp.dot(p_src[...], vaug_scr[rows(j), :], preferred_element_type=jnp.float32)
        al = al_src[...]
        acc_scr[slot] = acc_scr[slot] * jnp.concatenate([al, al], axis=1) + pv

    def iteration(par, a, b, c):
        if a is not None:
            stage_a(a[0], a[1], s_bufs[par])
        if b is not None:
            stage_b(b[0], b[1], s_bufs[1 - par], p_bufs[1 - par], al_bufs[1 - par])
        if c is not None:
            stage_c(c[0], c[1], p_bufs[par], al_bufs[par])

    def static_iteration(it):
        pick = lambda i: steps[i] if 0 <= i < n_steps else None
        iteration(it % 2, pick(it), pick(it - 1), pick(it - 2))
        c = pick(it - 2)
        if c is not None and c[0] == c[1]:
            finalize(c[0])

    def advance(qi, j):
        wrap = j == qi
        return jnp.where(wrap, qi + 1, qi), jnp.where(wrap, 0, j + 1)

    unroll = ATT_UNROLL
    assert unroll % 2 == 0 and ATT_ACC_SLOTS == 4
    first_q = unroll // 2 - 1
    lo = 2 + first_q * (first_q + 1) // 2
    lo += lo % 2
    n_trips = max(n_steps - lo, 0) // unroll
    loop_end = lo + unroll * n_trips if n_trips > 0 else 0
    if n_trips > 0:
        for it in range(lo):
            static_iteration(it)

        def trip(_, carry):
            a, b, c = carry[0:2], carry[2:4], carry[4:6]
            done1, done2 = jnp.bool_(False), jnp.bool_(False)
            q1, q2 = jnp.int32(0), jnp.int32(0)
            for k in range(unroll):
                iteration(k % 2, a, b, c)
                last = c[0] == c[1]
                first, second = last & ~done1, last & done1
                q1, q2 = jnp.where(first, c[0], q1), jnp.where(second, c[0], q2)
                done1, done2 = done1 | last, done2 | second
                a, b, c = advance(*a), a, b
            pl.when(done1)(lambda: finalize(q1))
            pl.when(done2)(lambda: finalize(q2))
            return (*a, *b, *c)

        init = tuple(jnp.int32(v) for i in (lo, lo - 1, lo - 2) for v in steps[i])
        lax.fori_loop(0, n_trips, trip, init)
    for it in range(loop_end, n_steps + 2):
        static_iteration(it)


def _attention(proj3, bias, subln_g, lq1, lk1, lq2, lk2, t):
    b, s, _ = proj3.shape
    hd = ATT_V_DIM
    kq, kk, kv, kg = COL_Q // hd, COL_K // hd, COL_V // hd, COL_GATT // hd
    vec = lambda n: pl.BlockSpec((1, n), lambda bi, h: (0, 0))
    head_cols = lambda first: pl.BlockSpec((None, s, hd), lambda bi, h: (bi, 0, first + h))
    return pl.pallas_call(
        functools.partial(_attn_kernel, t=t, nq=s // t),
        grid=(b, ATT_HEADS),
        in_specs=[
            head_cols(kq), head_cols(kk), head_cols(kv), head_cols(kg),
            pl.BlockSpec((None, N_BIAS_TILES, t, t), lambda bi, h: (h, 0, 0, 0)),
            vec(hd), vec(ATT_QK_DIM), vec(ATT_QK_DIM), vec(ATT_QK_DIM), vec(ATT_QK_DIM),
        ],
        out_specs=head_cols(0),
        out_shape=jax.ShapeDtypeStruct((b, s, ATT_WIDTH), jnp.bfloat16),
        scratch_shapes=[
            pltpu.VMEM((s, 2 * hd), jnp.bfloat16),
            pltpu.VMEM((2 * t, LANES), jnp.float32),
            pltpu.VMEM((ATT_ACC_SLOTS, 2 * t, 2 * hd), jnp.float32),
            pltpu.VMEM((2 * t, t), jnp.float32), pltpu.VMEM((2 * t, t), jnp.float32),
            pltpu.VMEM((2 * t, t), jnp.bfloat16), pltpu.VMEM((2 * t, t), jnp.bfloat16),
            pltpu.VMEM((2 * t, LANES), jnp.float32), pltpu.VMEM((2 * t, LANES), jnp.float32),
        ],
        compiler_params=pltpu.CompilerParams(
            dimension_semantics=("arbitrary", "arbitrary"),
            vmem_limit_bytes=VMEM_LIMIT_BYTES),
        name="diff_attn",
    )(proj3, proj3, proj3, proj3, bias, subln_g, lq1, lk1, lq2, lk2)


def _silu(x):
    h = 0.5 * x
    return h + h * jnp.tanh(h)


def _ssd_kernel(xbc_ref, z_ref, dt_ref, shift_ref, convw_ref, convb_ref, dtb_ref, alog_ref, dskip_ref, normg_ref,
                o_ref, tail_scr, xs_scr, xsb_scr, b_scr, c_scr, y_scr, st_scr):
    c = pl.program_id(1)
    L = SSM_CHUNK
    P, N, G, HG = SSM_HEAD_DIM, SSM_STATE, SSM_GROUPS, SSM_HEADS_PER_GROUP
    halo = SUBLANES
    f32, bf16 = jnp.float32, jnp.bfloat16

    @pl.when(c == 0)
    def _():
        tail_scr[...] = jnp.zeros(tail_scr.shape, f32)
        st_scr[...] = jnp.zeros(st_scr.shape, f32)

    ct = 512
    for t0 in range(0, SSM_CONV_DIM, ct):
        cs = slice(t0, t0 + ct)
        w = [convw_ref[k:k + 1, cs] for k in range(SSM_CONV)]
        u = xbc_ref[:, cs].astype(f32)
        taps = jnp.concatenate([(u * w[SSM_CONV - 1 - k]).astype(bf16) for k in range(1, SSM_CONV)], axis=0)
        acc = (convb_ref[:, cs] + w[SSM_CONV - 1] * u
               + jnp.dot(shift_ref[...], taps, preferred_element_type=f32))
        corr = sum(w[SSM_CONV - 1 - k] * tail_scr[halo - k:2 * halo - k, cs] for k in range(1, SSM_CONV))
        act = _silu(jnp.concatenate([acc[:halo] + corr, acc[halo:]], axis=0))
        tail_scr[:halo, cs] = u[L - halo:]
        if t0 < SSM_INNER:
            xs_scr[:, cs] = act
            xsb_scr[:, cs] = act.astype(bf16)
        elif t0 < SSM_INNER + G * N:
            b_scr[:, t0 - SSM_INNER:t0 - SSM_INNER + ct] = act.astype(bf16)
        else:
            c_scr[:, t0 - SSM_INNER - G * N:t0 - SSM_INNER - G * N + ct] = act.astype(bf16)

    xdt = dt_ref[...] + dtb_ref[...]
    dtv = jnp.maximum(xdt, 0.0) + jnp.log1p(jnp.exp(-jnp.abs(xdt)))
    adt = dtv * (-jnp.exp(alog_ref[...]))
    row = lax.broadcasted_iota(jnp.int32, (L, L), 0)
    col = lax.broadcasted_iota(jnp.int32, (L, L), 1)
    tril = row >= col
    tri = jnp.where(tril, 1.0, 0.0).astype(bf16)
    hi = adt.astype(bf16)
    r1 = adt - hi.astype(f32)
    mid = r1.astype(bf16)
    lo = (r1 - mid.astype(f32)).astype(bf16)
    a_cs = (jnp.dot(tri, hi, preferred_element_type=f32)
            + jnp.dot(tri, mid, preferred_element_type=f32)
            + jnp.dot(tri, lo, preferred_element_type=f32))
    a_last = a_cs[L - 1:L, :]
    exp_last = jnp.exp(a_last)
    a2 = a_cs * LOG2E
    src2_t = (a2 - jnp.log2(dtv)).T
    w_end_t = (jnp.exp(a_last - a_cs) * dtv).T

    lane = lax.broadcasted_iota(jnp.int32, (L, LANES), 1)
    lane_row = lax.broadcasted_iota(jnp.int32, (1, LANES), 1)
    n_lt = L // LANES
    tril_blk = (lax.broadcasted_iota(jnp.int32, (LANES, LANES), 0)
                >= lax.broadcasted_iota(jnp.int32, (LANES, LANES), 1))

    def decay_block(cb, a_bc2, src_row, bi, bj):
        if bj > bi:
            return jnp.zeros((LANES, LANES), bf16)
        rs, cs = slice(bi * LANES, (bi + 1) * LANES), slice(bj * LANES, (bj + 1) * LANES)
        seg = a_bc2[rs, :] - src_row[:, cs]
        if bj == bi:
            seg = jnp.where(tril_blk, seg, -jnp.inf)
        return (cb[rs, cs] * jnp.exp2(seg)).astype(bf16)

    for g in range(G):
        b_g = b_scr[:, g * N:(g + 1) * N]
        c_g = c_scr[:, g * N:(g + 1) * N]
        cb = lax.dot_general(c_g, b_g, (((1,), (1,)), ((), ())), preferred_element_type=f32)
        b_g_t = b_g.astype(f32).T
        st_g = st_scr[g]
        y_off = jnp.dot(c_g, st_g.astype(bf16), preferred_element_type=f32)
        for pr in range(HG // 2):
            h0 = g * HG + 2 * pr
            cols = slice(h0 * P, (h0 + 2) * P)
            lcols = slice(2 * pr * P, (2 * pr + 2) * P)
            m_parts, bt_parts, ea_parts = [], [], []
            for h in (h0, h0 + 1):
                a_bc2 = jnp.broadcast_to(a2[:, h:h + 1], (L, LANES))
                src_row = src2_t[h:h + 1, :]
                m_parts.append(jnp.concatenate(
                    [jnp.concatenate([decay_block(cb, a_bc2, src_row, bi, bj) for bj in range(n_lt)], axis=1)
                     for bi in range(n_lt)], axis=0))
                bt_parts.append((b_g_t * w_end_t[h:h + 1, :]).astype(bf16))
                ea_parts.append(jnp.exp2(a_bc2))
            xsb = xsb_scr[:, cols]
            zero = jnp.zeros_like(xsb)
            lane_b = lax.broadcasted_iota(jnp.int32, xsb.shape, 1)
            x_bd = jnp.concatenate([jnp.where(lane_b < P, xsb, zero),
                                    jnp.where(lane_b >= P, xsb, zero)], axis=0)
            y_pair = jnp.dot(jnp.concatenate(m_parts, axis=1), x_bd, preferred_element_type=f32)
            ea = jnp.where(lane < P, ea_parts[0], ea_parts[1])
            y_scr[:, cols] = y_pair + y_off[:, lcols] * ea + xs_scr[:, cols] * dskip_ref[:, cols]
            upd = jnp.dot(jnp.concatenate(bt_parts, axis=1), x_bd, preferred_element_type=f32)
            el = jnp.where(lane_row < P, exp_last[:, h0:h0 + 1], exp_last[:, h0 + 1:h0 + 2])
            st_scr[g, :, lcols] = st_g[:, lcols] * el + upd

    gw = SSM_INNER // G
    for g in range(G):
        gs = slice(g * gw, (g + 1) * gw)
        zz = z_ref[:, gs].astype(f32)
        y = y_scr[:, gs] * _silu(zz)
        ms = jnp.mean(y * y, axis=-1, keepdims=True)
        o_ref[:, gs] = (y * lax.rsqrt(ms + EPS) * normg_ref[:, gs]).astype(o_ref.dtype)


def _ssd(proj3, dt3, conv_w, conv_b, dt_bias, a_log, d_skip_ch, norm_g):
    b, s, _ = proj3.shape
    L = SSM_CHUNK
    assert s % L == 0
    full = lambda shape: pl.BlockSpec(shape, lambda bi, c: (0,) * len(shape))
    t_idx = np.arange(L)
    shift = np.concatenate([(t_idx[None, :] == t_idx[:, None] - k) for k in range(1, SSM_CONV)], axis=1)
    shift = jnp.asarray(shift, jnp.bfloat16)
    return pl.pallas_call(
        _ssd_kernel,
        grid=(b, s // L),
        in_specs=[
            pl.BlockSpec((None, L, SSM_CONV_DIM), lambda bi, c: (bi, c, COL_XBC // SSM_CONV_DIM)),
            pl.BlockSpec((None, L, SSM_INNER), lambda bi, c: (bi, c, COL_Z // SSM_INNER)),
            pl.BlockSpec((None, L, LANES), lambda bi, c: (bi, c, 0)),
            full((L, (SSM_CONV - 1) * L)),
            full((SSM_CONV, SSM_CONV_DIM)), full((1, SSM_CONV_DIM)),
            full((1, LANES)), full((1, LANES)), full((1, SSM_INNER)), full((1, SSM_INNER)),
        ],
        out_specs=pl.BlockSpec((None, L, SSM_INNER), lambda bi, c: (bi, c, 0)),
        out_shape=jax.ShapeDtypeStruct((b, s, SSM_INNER), jnp.bfloat16),
        scratch_shapes=[
            pltpu.VMEM((2 * SUBLANES, SSM_CONV_DIM), jnp.float32),
            pltpu.VMEM((L, SSM_INNER), jnp.float32),
            pltpu.VMEM((L, SSM_INNER), jnp.bfloat16),
            pltpu.VMEM((L, SSM_GROUPS * SSM_STATE), jnp.bfloat16),
            pltpu.VMEM((L, SSM_GROUPS * SSM_STATE), jnp.bfloat16),
            pltpu.VMEM((L, SSM_INNER), jnp.float32),
            pltpu.VMEM((SSM_GROUPS, SSM_STATE, SSM_HEADS_PER_GROUP * SSM_HEAD_DIM), jnp.float32),
        ],
        compiler_params=pltpu.CompilerParams(
            dimension_semantics=("arbitrary", "arbitrary"),
            vmem_limit_bytes=VMEM_LIMIT_BYTES),
        name="ssd",
    )(proj3, proj3, dt3, shift, conv_w, conv_b, dt_bias, a_log, d_skip_ch, norm_g)


def _out_kernel(x_ref, oatt_ref, yssm_ref, ga_ref, gs_ref, watt_ref, wssm_ref, wout_ref, gpost_ref, o_ref):
    y_att = jnp.dot(oatt_ref[...], watt_ref[...], preferred_element_type=jnp.float32)
    y_ssm = jnp.dot(yssm_ref[...], wssm_ref[...], preferred_element_type=jnp.float32)
    ga = ga_ref[...].astype(jnp.float32)
    gs = gs_ref[...].astype(jnp.float32)
    mixed = y_att / (1.0 + jnp.exp(-ga)) + y_ssm / (1.0 + jnp.exp(-gs))
    out = jnp.dot(mixed.astype(jnp.bfloat16), wout_ref[...], preferred_element_type=jnp.float32)
    ms = jnp.mean(out * out, axis=-1, keepdims=True)
    o_ref[...] = x_ref[...] + out * lax.rsqrt(ms + EPS) * gpost_ref[...]


def _out_proj(x2, o_att, y_ssm, proj2, w_att, w_ssm, w_out, g_post):
    m = x2.shape[0]
    tm = min(OUT_TM, m)
    const = lambda shape: pl.BlockSpec(shape, lambda i: (0, 0))
    return pl.pallas_call(
        _out_kernel,
        grid=(m // tm,),
        in_specs=[
            pl.BlockSpec((tm, D_MODEL), lambda i: (i, 0)),
            pl.BlockSpec((tm, ATT_WIDTH), lambda i: (i, 0)),
            pl.BlockSpec((tm, SSM_INNER), lambda i: (i, 0)),
            pl.BlockSpec((tm, D_MODEL), lambda i: (i, COL_GATE // D_MODEL)),
            pl.BlockSpec((tm, D_MODEL), lambda i: (i, COL_GATE // D_MODEL + 1)),
            const((ATT_WIDTH, D_MODEL)), const((SSM_INNER, D_MODEL)), const((D_MODEL, D_MODEL)),
            const((1, D_MODEL)),
        ],
        out_specs=pl.BlockSpec((tm, D_MODEL), lambda i: (i, 0)),
        out_shape=jax.ShapeDtypeStruct((m, D_MODEL), jnp.float32),
        compiler_params=pltpu.CompilerParams(
            dimension_semantics=("arbitrary",),
            vmem_limit_bytes=VMEM_LIMIT_BYTES),
        name="out_proj",
    )(x2, o_att, y_ssm, proj2, proj2, w_att, w_ssm, w_out, g_post)


def kernel(x, g_pre, w_in, att_lambda_q1, att_lambda_k1, att_lambda_q2, att_lambda_k2, att_subln_g, rel_bias,
           conv_w, conv_b, dt_bias, a_log, d_skip, ssm_norm_g, w_att_proj, w_ssm_proj, w_out, g_post):
    b, s, d = x.shape
    assert d == D_MODEL and g_pre.shape[0] == 1
    m = b * s
    f32, bf16 = jnp.float32, jnp.bfloat16
    t = min(ATT_BLOCK, s)
    assert s % t == 0 and s % SSM_CHUNK == 0

    w = w_in[0]
    o_q, o_k, o_v, o_g = 0, ATT_WIDTH, 2 * ATT_WIDTH, 3 * ATT_WIDTH
    o_z = 4 * ATT_WIDTH
    o_xbc = o_z + SSM_INNER
    o_dt = o_xbc + SSM_CONV_DIM
    o_gate = o_dt + SSM_HEADS
    w_main = jnp.concatenate([w[:, o_q:o_z], w[:, o_xbc:o_dt], w[:, o_z:o_xbc], w[:, o_gate:]], axis=1).astype(bf16)
    w_dt = jnp.pad(w[:, o_dt:o_gate], ((0, 0), (0, LANES - SSM_HEADS))).astype(bf16)
    col_scale = jnp.concatenate([jnp.full((1, ATT_WIDTH), ATT_QK_DIM ** -0.5 * LOG2E, f32),
                                 jnp.ones((1, PROJ_COLS - ATT_WIDTH), f32)], axis=1)

    x2 = x.reshape(m, d)
    proj, dt_raw = _in_proj(x2, g_pre, w_main, w_dt, col_scale)
    proj3 = proj.reshape(b, s, PROJ_COLS)
    dt3 = dt_raw.reshape(b, s, LANES)

    bias = _bias_tiles(rel_bias, t)
    o_att = _attention(proj3, bias, att_subln_g, att_lambda_q1, att_lambda_k1, att_lambda_q2, att_lambda_k2, t)

    pad_h = lambda v: jnp.pad(v, ((0, 0), (0, LANES - SSM_HEADS)))
    d_skip_ch = jnp.repeat(d_skip, SSM_HEAD_DIM, axis=1)
    y_ssm = _ssd(proj3, dt3, conv_w[0], conv_b, pad_h(dt_bias), pad_h(a_log), d_skip_ch, ssm_norm_g)

    out = _out_proj(x2, o_att.reshape(m, ATT_WIDTH), y_ssm.reshape(m, SSM_INNER), proj,
                    w_att_proj[0].astype(bf16), w_ssm_proj[0].astype(bf16), w_out[0].astype(bf16), g_post)
    return out.reshape(b, s, d)
```

```python
import functools
import math

import numpy as np
import jax
import jax.numpy as jnp
from jax import lax
from jax.experimental import pallas as pl
from jax.experimental.pallas import tpu as pltpu

D_MODEL = 1024
ATT_HEADS = 8
ATT_QK_DIM = 64
ATT_V_DIM = 2 * ATT_QK_DIM
ATT_WIDTH = ATT_HEADS * ATT_V_DIM
REL_BUCKETS = 32
REL_MAX_DIST = 128
SSM_INNER = 2 * D_MODEL
SSM_HEAD_DIM = 64
SSM_HEADS = SSM_INNER // SSM_HEAD_DIM
SSM_GROUPS = 8
SSM_HEADS_PER_GROUP = SSM_HEADS // SSM_GROUPS
SSM_STATE = 128
SSM_CONV = 4
SSM_CHUNK = 256
SSM_CONV_DIM = SSM_INNER + 2 * SSM_GROUPS * SSM_STATE
EPS = 1e-6
LAMBDA_INIT = 0.8 - 0.6 * math.exp(-0.3 * 0)

LANES = 128
SUBLANES = 8
VMEM_LIMIT_BYTES = 56 * 1024 * 1024
LOG2E = math.log2(math.e)
NEG_BIG = -1e30

COL_Q = 0
COL_K = COL_Q + ATT_WIDTH
COL_V = COL_K + ATT_WIDTH
COL_GATT = COL_V + ATT_WIDTH
COL_XBC = COL_GATT + ATT_WIDTH
COL_Z = COL_XBC + SSM_CONV_DIM
COL_GATE = COL_Z + SSM_INNER
PROJ_COLS = COL_GATE + 2 * D_MODEL

ATT_BLOCK = 512
N_BIAS_TILES = 3
ATT_SOFTMAX_ROWS = 64
ATT_ROW_GROUP = 16
ATT_UNROLL = 8
ATT_ACC_SLOTS = 4
PROJ_TM = 1024
PROJ_TN = 3072
OUT_TM = 512


def _t5_bucket_thresholds():
    max_exact = REL_BUCKETS // 2
    n = np.arange(0, 4 * REL_MAX_DIST)
    nf = np.maximum(n, 1).astype(np.float32)
    large = max_exact + (np.log(nf / np.float32(max_exact)) / np.float32(math.log(REL_MAX_DIST / max_exact))
                         * np.float32(REL_BUCKETS - max_exact)).astype(np.int32)
    large = np.minimum(large, REL_BUCKETS - 1)
    bucket = np.where(n < max_exact, n, large)
    assert np.all(np.diff(bucket) >= 0)
    thr = [int(np.argmax(bucket >= b)) for b in range(REL_BUCKETS)]
    assert bucket[thr[-1]] == REL_BUCKETS - 1
    return thr


_BUCKET_THR = _t5_bucket_thresholds()


def _in_proj_kernel(x_ref, g_ref, w_ref, wdt_ref, scale_ref, o_ref, dt_ref, h_scr):
    @pl.when(pl.program_id(1) == 0)
    def _():
        xf = x_ref[...]
        ms = jnp.mean(xf * xf, axis=-1, keepdims=True)
        h = (xf * lax.rsqrt(ms + EPS) * g_ref[...]).astype(jnp.bfloat16)
        h_scr[...] = h
        dt_ref[...] = jnp.dot(h, wdt_ref[...], preferred_element_type=jnp.float32)

    acc = jnp.dot(h_scr[...], w_ref[...], preferred_element_type=jnp.float32)
    o_ref[...] = (acc * scale_ref[...]).astype(o_ref.dtype)


def _in_proj(x2, g_pre, w_main, w_dt, col_scale):
    m = x2.shape[0]
    tm = min(PROJ_TM, m)
    tn = PROJ_TN
    return pl.pallas_call(
        _in_proj_kernel,
        grid=(m // tm, PROJ_COLS // tn),
        in_specs=[
            pl.BlockSpec((tm, D_MODEL), lambda i, j: (i, 0)),
            pl.BlockSpec((1, D_MODEL), lambda i, j: (0, 0)),
            pl.BlockSpec((D_MODEL, tn), lambda i, j: (0, j)),
            pl.BlockSpec((D_MODEL, LANES), lambda i, j: (0, 0)),
            pl.BlockSpec((1, tn), lambda i, j: (0, j)),
        ],
        out_specs=[
            pl.BlockSpec((tm, tn), lambda i, j: (i, j)),
            pl.BlockSpec((tm, LANES), lambda i, j: (i, 0)),
        ],
        out_shape=[
            jax.ShapeDtypeStruct((m, PROJ_COLS), jnp.bfloat16),
            jax.ShapeDtypeStruct((m, LANES), jnp.float32),
        ],
        scratch_shapes=[pltpu.VMEM((tm, D_MODEL), jnp.bfloat16)],
        compiler_params=pltpu.CompilerParams(
            dimension_semantics=("arbitrary", "arbitrary"),
            vmem_limit_bytes=VMEM_LIMIT_BYTES),
        name="in_proj",
    )(x2, g_pre, w_main, w_dt, col_scale)


def _bias_kernel(rb_ref, o_ref, *, t):
    h = pl.program_id(0)
    row = lax.broadcasted_iota(jnp.int32, (t, t), 0)
    col = lax.broadcasted_iota(jnp.int32, (t, t), 1)
    far = rb_ref[REL_BUCKETS - 1, h]
    for d in range(N_BIAS_TILES):
        dist = d * t + row - col
        val = jnp.full((t, t), rb_ref[0, h], jnp.float32)
        for b in range(1, REL_BUCKETS):
            val = jnp.where(dist >= _BUCKET_THR[b], rb_ref[b, h], val)
        val = (val - far) * LOG2E
        if d == 0:
            val = jnp.where(dist >= 0, val, NEG_BIG)
        o_ref[d] = val


def _bias_tiles(rel_bias, t):
    assert t + 1 >= _BUCKET_THR[-1]
    return pl.pallas_call(
        functools.partial(_bias_kernel, t=t),
        grid=(ATT_HEADS,),
        in_specs=[pl.BlockSpec(memory_space=pltpu.SMEM)],
        out_specs=pl.BlockSpec((None, N_BIAS_TILES, t, t), lambda h: (h, 0, 0, 0)),
        out_shape=jax.ShapeDtypeStruct((ATT_HEADS, N_BIAS_TILES, t, t), jnp.float32),
        compiler_params=pltpu.CompilerParams(dimension_semantics=("arbitrary",)),
        name="t5_bias",
    )(rel_bias)


def _attn_kernel(q_ref, k_ref, v_ref, gatt_ref, bias_ref, subln_ref, lq1_ref, lk1_ref, lq2_ref, lk2_ref,
                 o_ref, vaug_scr, m_scr, acc_scr, s0_scr, s1_scr, p0_scr, p1_scr, al0_scr, al1_scr, *, t, nq):
    dv = ATT_V_DIM
    s_bufs, p_bufs, al_bufs = (s0_scr, s1_scr), (p0_scr, p1_scr), (al0_scr, al1_scr)
    steps = [(qi, j) for qi in range(nq) for j in range(qi + 1)]
    n_steps = len(steps)

    vaug_scr[:, :dv] = v_ref[...]
    vaug_scr[:, dv:] = jnp.ones((v_ref.shape[0], dv), v_ref.dtype)
    acc_scr[...] = jnp.zeros(acc_scr.shape, jnp.float32)
    m_scr[...] = jnp.full(m_scr.shape, -jnp.inf, jnp.float32)

    def rows(idx):
        if isinstance(idx, int):
            return pl.ds(idx * t, t)
        return pl.ds(pl.multiple_of(idx * t, t), t)

    def stage_a(qi, j, s_dst):
        q = q_ref[rows(qi), :]
        lane = lax.broadcasted_iota(jnp.int32, q.shape, 1)
        zero = jnp.zeros_like(q)
        grp = (t // ATT_ROW_GROUP, ATT_ROW_GROUP)
        qs = jnp.stack([jnp.where(lane < ATT_QK_DIM, q, zero).reshape(*grp, q.shape[1]),
                        jnp.where(lane >= ATT_QK_DIM, q, zero).reshape(*grp, q.shape[1])],
                       axis=1).reshape(2 * t, q.shape[1])
        d = min(qi - j, N_BIAS_TILES - 1) if isinstance(qi, int) else jnp.minimum(qi - j, N_BIAS_TILES - 1)
        s = lax.dot_general(qs, k_ref[rows(j), :], (((1,), (1,)), ((), ())),
                            preferred_element_type=jnp.float32)
        s_dst[...] = (s.reshape(grp[0], 2, grp[1], t)
                      + bias_ref[d].reshape(grp[0], 1, grp[1], t)).reshape(2 * t, t)

    def stage_b(qi, j, s_src, p_dst, al_dst):
        for r0 in range(0, 2 * t, ATT_SOFTMAX_ROWS):
            rs = slice(r0, r0 + ATT_SOFTMAX_ROWS)
            m_old = m_scr[rs, :]
            if isinstance(j, int):
                m_prev = jnp.full(m_old.shape, -jnp.inf, jnp.float32) if j == 0 else m_old
            else:
                m_prev = jnp.where(j == 0, -jnp.inf, m_old)
            m_new = jnp.maximum(m_prev, jnp.max(s_src[rs, :], axis=-1, keepdims=True))
            al_dst[rs, :] = jnp.exp2(m_prev - m_new)
            m_scr[rs, :] = m_new
        for r0 in range(0, 2 * t, ATT_SOFTMAX_ROWS):
            rs = slice(r0, r0 + ATT_SOFTMAX_ROWS)
            m_rep = jnp.concatenate([m_scr[rs, :]] * (t // LANES), axis=1)
            p_dst[rs, :] = jnp.exp2(s_src[rs, :] - m_rep).astype(jnp.bfloat16)

    def finalize(qi):
        acc = acc_scr[qi % ATT_ACC_SLOTS] if isinstance(qi, int) else acc_scr[qi & (ATT_ACC_SLOTS - 1)]
        o = acc[:, :dv] / acc[:, dv:]
        o = o.reshape(t // ATT_ROW_GROUP, 2, ATT_ROW_GROUP, dv)
        lam = (jnp.exp(jnp.sum(lq1_ref[...] * lk1_ref[...], axis=-1, keepdims=True))
               - jnp.exp(jnp.sum(lq2_ref[...] * lk2_ref[...], axis=-1, keepdims=True))
               + LAMBDA_INIT)
        o = o[:, 0].reshape(t, dv) - lam * o[:, 1].reshape(t, dv)
        ms = jnp.mean(o * o, axis=-1, keepdims=True)
        o = o * lax.rsqrt(ms + EPS) * subln_ref[...] * (1.0 - LAMBDA_INIT)
        o = o * _silu(gatt_ref[rows(qi), :].astype(jnp.float32))
        o_ref[rows(qi), :] = o.astype(o_ref.dtype)

    def stage_c(qi, j, p_src, al_src):
        slot = qi % ATT_ACC_SLOTS if isinstance(qi, int) else qi & (ATT_ACC_SLOTS - 1)
        pv = jnp.dot(p_src[...], vaug_scr[rows(j), :], preferred_element_type=jnp.float32)
        al = al_src[...]
        acc_scr[slot] = acc_scr[slot] * jnp.concatenate([al, al], axis=1) + pv

    def iteration(par, a, b, c):
        if a is not None:
            stage_a(a[0], a[1], s_bufs[par])
        if b is not None:
            stage_b(b[0], b[1], s_bufs[1 - par], p_bufs[1 - par], al_bufs[1 - par])
        if c is not None:
            stage_c(c[0], c[1], p_bufs[par], al_bufs[par])

    def static_iteration(it):
        pick = lambda i: steps[i] if 0 <= i < n_steps else None
        iteration(it % 2, pick(it), pick(it - 1), pick(it - 2))
        c = pick(it - 2)
        if c is not None and c[0] == c[1]:
            finalize(c[0])

    def advance(qi, j):
        wrap = j == qi
        return jnp.where(wrap, qi + 1, qi), jnp.where(wrap, 0, j + 1)

    unroll = ATT_UNROLL
    assert unroll % 2 == 0 and ATT_ACC_SLOTS == 4
    first_q = unroll // 2 - 1
    lo = 2 + first_q * (first_q + 1) // 2
    lo += lo % 2
    n_trips = max(n_steps - lo, 0) // unroll
    loop_end = lo + unroll * n_trips if n_trips > 0 else 0
    if n_trips > 0:
        for it in range(lo):
            static_iteration(it)

        def trip(_, carry):
            a, b, c = carry[0:2], carry[2:4], carry[4:6]
            done1, done2 = jnp.bool_(False), jnp.bool_(False)
            q1, q2 = jnp.int32(0), jnp.int32(0)
            for k in range(unroll):
                iteration(k % 2, a, b, c)
                last = c[0] == c[1]
                first, second = last & ~done1, last & done1
                q1, q2 = jnp.where(first, c[0], q1), jnp.where(second, c[0], q2)
                done1, done2 = done1 | last, done2 | second
                a, b, c = advance(*a), a, b
            pl.when(done1)(lambda: finalize(q1))
            pl.when(done2)(lambda: finalize(q2))
            return (*a, *b, *c)

        init = tuple(jnp.int32(v) for i in (lo, lo - 1, lo - 2) for v in steps[i])
        lax.fori_loop(0, n_trips, trip, init)
    for it in range(loop_end, n_steps + 2):
        static_iteration(it)


def _attention(proj3, bias, subln_g, lq1, lk1, lq2, lk2, t):
    b, s, _ = proj3.shape
    hd = ATT_V_DIM
    kq, kk, kv, kg = COL_Q // hd, COL_K // hd, COL_V // hd, COL_GATT // hd
    vec = lambda n: pl.BlockSpec((1, n), lambda bi, h: (0, 0))
    head_cols = lambda first: pl.BlockSpec((None, s, hd), lambda bi, h: (bi, 0, first + h))
    return pl.pallas_call(
        functools.partial(_attn_kernel, t=t, nq=s // t),
        grid=(b, ATT_HEADS),
        in_specs=[
            head_cols(kq), head_cols(kk), head_cols(kv), head_cols(kg),
            pl.BlockSpec((None, N_BIAS_TILES, t, t), lambda bi, h: (h, 0, 0, 0)),
            vec(hd), vec(ATT_QK_DIM), vec(ATT_QK_DIM), vec(ATT_QK_DIM), vec(ATT_QK_DIM),
        ],
        out_specs=head_cols(0),
        out_shape=jax.ShapeDtypeStruct((b, s, ATT_WIDTH), jnp.bfloat16),
        scratch_shapes=[
            pltpu.VMEM((s, 2 * hd), jnp.bfloat16),
            pltpu.VMEM((2 * t, LANES), jnp.float32),
            pltpu.VMEM((ATT_ACC_SLOTS, 2 * t, 2 * hd), jnp.float32),
            pltpu.VMEM((2 * t, t), jnp.float32), pltpu.VMEM((2 * t, t), jnp.float32),
            pltpu.VMEM((2 * t, t), jnp.bfloat16), pltpu.VMEM((2 * t, t), jnp.bfloat16),
            pltpu.VMEM((2 * t, LANES), jnp.float32), pltpu.VMEM((2 * t, LANES), jnp.float32),
        ],
        compiler_params=pltpu.CompilerParams(
            dimension_semantics=("arbitrary", "arbitrary"),
            vmem_limit_bytes=VMEM_LIMIT_BYTES),
        name="diff_attn",
    )(proj3, proj3, proj3, proj3, bias, subln_g, lq1, lk1, lq2, lk2)


def _silu(x):
    h = 0.5 * x
    return h + h * jnp.tanh(h)


def _ssd_kernel(xbc_ref, z_ref, dt_ref, shift_ref, convw_ref, convb_ref, dtb_ref, alog_ref, dskip_ref, normg_ref,
                o_ref, tail_scr, xs_scr, xsb_scr, b_scr, c_scr, y_scr, st_scr):
    c = pl.program_id(1)
    L = SSM_CHUNK
    P, N, G, HG = SSM_HEAD_DIM, SSM_STATE, SSM_GROUPS, SSM_HEADS_PER_GROUP
    halo = SUBLANES
    f32, bf16 = jnp.float32, jnp.bfloat16

    @pl.when(c == 0)
    def _():
        tail_scr[...] = jnp.zeros(tail_scr.shape, f32)
        st_scr[...] = jnp.zeros(st_scr.shape, f32)

    ct = 512
    for t0 in range(0, SSM_CONV_DIM, ct):
        cs = slice(t0, t0 + ct)
        w = [convw_ref[k:k + 1, cs] for k in range(SSM_CONV)]
        u = xbc_ref[:, cs].astype(f32)
        taps = jnp.concatenate([(u * w[SSM_CONV - 1 - k]).astype(bf16) for k in range(1, SSM_CONV)], axis=0)
        acc = (convb_ref[:, cs] + w[SSM_CONV - 1] * u
               + jnp.dot(shift_ref[...], taps, preferred_element_type=f32))
        corr = sum(w[SSM_CONV - 1 - k] * tail_scr[halo - k:2 * halo - k, cs] for k in range(1, SSM_CONV))
        act = _silu(jnp.concatenate([acc[:halo] + corr, acc[halo:]], axis=0))
        tail_scr[:halo, cs] = u[L - halo:]
        if t0 < SSM_INNER:
            xs_scr[:, cs] = act
            xsb_scr[:, cs] = act.astype(bf16)
        elif t0 < SSM_INNER + G * N:
            b_scr[:, t0 - SSM_INNER:t0 - SSM_INNER + ct] = act.astype(bf16)
        else:
            c_scr[:, t0 - SSM_INNER - G * N:t0 - SSM_INNER - G * N + ct] = act.astype(bf16)

    xdt = dt_ref[...] + dtb_ref[...]
    dtv = jnp.maximum(xdt, 0.0) + jnp.log1p(jnp.exp(-jnp.abs(xdt)))
    adt = dtv * (-jnp.exp(alog_ref[...]))
    row = lax.broadcasted_iota(jnp.int32, (L, L), 0)
    col = lax.broadcasted_iota(jnp.int32, (L, L), 1)
    tril = row >= col
    tri = jnp.where(tril, 1.0, 0.0).astype(bf16)
    hi = adt.astype(bf16)
    r1 = adt - hi.astype(f32)
    mid = r1.astype(bf16)
    lo = (r1 - mid.astype(f32)).astype(bf16)
    a_cs = (jnp.dot(tri, hi, preferred_element_type=f32)
            + jnp.dot(tri, mid, preferred_element_type=f32)
            + jnp.dot(tri, lo, preferred_element_type=f32))
    a_last = a_cs[L - 1:L, :]
    exp_last = jnp.exp(a_last)
    a2 = a_cs * LOG2E
    src2_t = (a2 - jnp.log2(dtv)).T
    w_end_t = (jnp.exp(a_last - a_cs) * dtv).T

    lane = lax.broadcasted_iota(jnp.int32, (L, LANES), 1)
    lane_row = lax.broadcasted_iota(jnp.int32, (1, LANES), 1)
    n_lt = L // LANES
    tril_blk = (lax.broadcasted_iota(jnp.int32, (LANES, LANES), 0)
                >= lax.broadcasted_iota(jnp.int32, (LANES, LANES), 1))

    def decay_block(cb, a_bc2, src_row, bi, bj):
        if bj > bi:
            return jnp.zeros((LANES, LANES), bf16)
        rs, cs = slice(bi * LANES, (bi + 1) * LANES), slice(bj * LANES, (bj + 1) * LANES)
        seg = a_bc2[rs, :] - src_row[:, cs]
        if bj == bi:
            seg = jnp.where(tril_blk, seg, -jnp.inf)
        return (cb[rs, cs] * jnp.exp2(seg)).astype(bf16)

    for g in range(G):
        b_g = b_scr[:, g * N:(g + 1) * N]
        c_g = c_scr[:, g * N:(g + 1) * N]
        cb = lax.dot_general(c_g, b_g, (((1,), (1,)), ((), ())), preferred_element_type=f32)
        b_g_t = b_g.astype(f32).T
        st_g = st_scr[g]
        y_off = jnp.dot(c_g, st_g.astype(bf16), preferred_element_type=f32)
        for pr in range(HG // 2):
            h0 = g * HG + 2 * pr
            cols = slice(h0 * P, (h0 + 2) * P)
            lcols = slice(2 * pr * P, (2 * pr + 2) * P)
            m_parts, bt_parts, ea_parts = [], [], []
            for h in (h0, h0 + 1):
                a_bc2 = jnp.broadcast_to(a2[:, h:h + 1], (L, LANES))
                src_row = src2_t[h:h + 1, :]
                m_parts.append(jnp.concatenate(
                    [jnp.concatenate([decay_block(cb, a_bc2, src_row, bi, bj) for bj in range(n_lt)], axis=1)
                     for bi in range(n_lt)], axis=0))
                bt_parts.append((b_g_t * w_end_t[h:h + 1, :]).astype(bf16))
                ea_parts.append(jnp.exp2(a_bc2))
            xsb = xsb_scr[:, cols]
            zero = jnp.zeros_like(xsb)
            lane_b = lax.broadcasted_iota(jnp.int32, xsb.shape, 1)
            x_bd = jnp.concatenate([jnp.where(lane_b < P, xsb, zero),
                                    jnp.where(lane_b >= P, xsb, zero)], axis=0)
            y_pair = jnp.dot(jnp.concatenate(m_parts, axis=1), x_bd, preferred_element_type=f32)
            ea = jnp.where(lane < P, ea_parts[0], ea_parts[1])
            y_scr[:, cols] = y_pair + y_off[:, lcols] * ea + xs_scr[:, cols] * dskip_ref[:, cols]
            upd = jnp.dot(jnp.concatenate(bt_parts, axis=1), x_bd, preferred_element_type=f32)
            el = jnp.where(lane_row < P, exp_last[:, h0:h0 + 1], exp_last[:, h0 + 1:h0 + 2])
            st_scr[g, :, lcols] = st_g[:, lcols] * el + upd

    gw = SSM_INNER // G
    for g in range(G):
        gs = slice(g * gw, (g + 1) * gw)
        zz = z_ref[:, gs].astype(f32)
        y = y_scr[:, gs] * _silu(zz)
        ms = jnp.mean(y * y, axis=-1, keepdims=True)
        o_ref[:, gs] = (y * lax.rsqrt(ms + EPS) * normg_ref[:, gs]).astype(o_ref.dtype)


def _ssd(proj3, dt3, conv_w, conv_b, dt_bias, a_log, d_skip_ch, norm_g):
    b, s, _ = proj3.shape
    L = SSM_CHUNK
    assert s % L == 0
    full = lambda shape: pl.BlockSpec(shape, lambda bi, c: (0,) * len(shape))
    t_idx = np.arange(L)
    shift = np.concatenate([(t_idx[None, :] == t_idx[:, None] - k) for k in range(1, SSM_CONV)], axis=1)
    shift = jnp.asarray(shift, jnp.bfloat16)
    return pl.pallas_call(
        _ssd_kernel,
        grid=(b, s // L),
        in_specs=[
            pl.BlockSpec((None, L, SSM_CONV_DIM), lambda bi, c: (bi, c, COL_XBC // SSM_CONV_DIM)),
            pl.BlockSpec((None, L, SSM_INNER), lambda bi, c: (bi, c, COL_Z // SSM_INNER)),
            pl.BlockSpec((None, L, LANES), lambda bi, c: (bi, c, 0)),
            full((L, (SSM_CONV - 1) * L)),
            full((SSM_CONV, SSM_CONV_DIM)), full((1, SSM_CONV_DIM)),
            full((1, LANES)), full((1, LANES)), full((1, SSM_INNER)), full((1, SSM_INNER)),
        ],
        out_specs=pl.BlockSpec((None, L, SSM_INNER), lambda bi, c: (bi, c, 0)),
        out_shape=jax.ShapeDtypeStruct((b, s, SSM_INNER), jnp.bfloat16),
        scratch_shapes=[
            pltpu.VMEM((2 * SUBLANES, SSM_CONV_DIM), jnp.float32),
            pltpu.VMEM((L, SSM_INNER), jnp.float32),
            pltpu.VMEM((L, SSM_INNER), jnp.bfloat16),
            pltpu.VMEM((L, SSM_GROUPS * SSM_STATE), jnp.bfloat16),
            pltpu.VMEM((L, SSM_GROUPS * SSM_STATE), jnp.bfloat16),
            pltpu.VMEM((L, SSM_INNER), jnp.float32),
            pltpu.VMEM((SSM_GROUPS, SSM_STATE, SSM_HEADS_PER_GROUP * SSM_HEAD_DIM), jnp.float32),
        ],
        compiler_params=pltpu.CompilerParams(
            dimension_semantics=("arbitrary", "arbitrary"),
            vmem_limit_bytes=VMEM_LIMIT_BYTES),
        name="ssd",
    )(proj3, proj3, dt3, shift, conv_w, conv_b, dt_bias, a_log, d_skip_ch, norm_g)


def _out_kernel(x_ref, oatt_ref, yssm_ref, ga_ref, gs_ref, watt_ref, wssm_ref, wout_ref, gpost_ref, o_ref):
    y_att = jnp.dot(oatt_ref[...], watt_ref[...], preferred_element_type=jnp.float32)
    y_ssm = jnp.dot(yssm_ref[...], wssm_ref[...], preferred_element_type=jnp.float32)
    ga = ga_ref[...].astype(jnp.float32)
    gs = gs_ref[...].astype(jnp.float32)
    mixed = y_att / (1.0 + jnp.exp(-ga)) + y_ssm / (1.0 + jnp.exp(-gs))
    out = jnp.dot(mixed.astype(jnp.bfloat16), wout_ref[...], preferred_element_type=jnp.float32)
    ms = jnp.mean(out * out, axis=-1, keepdims=True)
    o_ref[...] = x_ref[...] + out * lax.rsqrt(ms + EPS) * gpost_ref[...]


def _out_proj(x2, o_att, y_ssm, proj2, w_att, w_ssm, w_out, g_post):
    m = x2.shape[0]
    tm = min(OUT_TM, m)
    const = lambda shape: pl.BlockSpec(shape, lambda i: (0, 0))
    return pl.pallas_call(
        _out_kernel,
        grid=(m // tm,),
        in_specs=[
            pl.BlockSpec((tm, D_MODEL), lambda i: (i, 0)),
            pl.BlockSpec((tm, ATT_WIDTH), lambda i: (i, 0)),
            pl.BlockSpec((tm, SSM_INNER), lambda i: (i, 0)),
            pl.BlockSpec((tm, D_MODEL), lambda i: (i, COL_GATE // D_MODEL)),
            pl.BlockSpec((tm, D_MODEL), lambda i: (i, COL_GATE // D_MODEL + 1)),
            const((ATT_WIDTH, D_MODEL)), const((SSM_INNER, D_MODEL)), const((D_MODEL, D_MODEL)),
            const((1, D_MODEL)),
        ],
        out_specs=pl.BlockSpec((tm, D_MODEL), lambda i: (i, 0)),
        out_shape=jax.ShapeDtypeStruct((m, D_MODEL), jnp.float32),
        compiler_params=pltpu.CompilerParams(
            dimension_semantics=("arbitrary",),
            vmem_limit_bytes=VMEM_LIMIT_BYTES),
        name="out_proj",
    )(x2, o_att, y_ssm, proj2, proj2, w_att, w_ssm, w_out, g_post)


def kernel(x, g_pre, w_in, att_lambda_q1, att_lambda_k1, att_lambda_q2, att_lambda_k2, att_subln_g, rel_bias,
           conv_w, conv_b, dt_bias, a_log, d_skip, ssm_norm_g, w_att_proj, w_ssm_proj, w_out, g_post):
    b, s, d = x.shape
    assert d == D_MODEL and g_pre.shape[0] == 1
    m = b * s
    f32, bf16 = jnp.float32, jnp.bfloat16
    t = min(ATT_BLOCK, s)
    assert s % t == 0 and s % SSM_CHUNK == 0

    w = w_in[0]
    o_q, o_k, o_v, o_g = 0, ATT_WIDTH, 2 * ATT_WIDTH, 3 * ATT_WIDTH
    o_z = 4 * ATT_WIDTH
    o_xbc = o_z + SSM_INNER
    o_dt = o_xbc + SSM_CONV_DIM
    o_gate = o_dt + SSM_HEADS
    w_main = jnp.concatenate([w[:, o_q:o_z], w[:, o_xbc:o_dt], w[:, o_z:o_xbc], w[:, o_gate:]], axis=1).astype(bf16)
    w_dt = jnp.pad(w[:, o_dt:o_gate], ((0, 0), (0, LANES - SSM_HEADS))).astype(bf16)
    col_scale = jnp.concatenate([jnp.full((1, ATT_WIDTH), ATT_QK_DIM ** -0.5 * LOG2E, f32),
                                 jnp.ones((1, PROJ_COLS - ATT_WIDTH), f32)], axis=1)

    x2 = x.reshape(m, d)
    proj, dt_raw = _in_proj(x2, g_pre, w_main, w_dt, col_scale)
    proj3 = proj.reshape(b, s, PROJ_COLS)
    dt3 = dt_raw.reshape(b, s, LANES)

    bias = _bias_tiles(rel_bias, t)
    o_att = _attention(proj3, bias, att_subln_g, att_lambda_q1, att_lambda_k1, att_lambda_q2, att_lambda_k2, t)

    pad_h = lambda v: jnp.pad(v, ((0, 0), (0, LANES - SSM_HEADS)))
    d_skip_ch = jnp.repeat(d_skip, SSM_HEAD_DIM, axis=1)
    y_ssm = _ssd(proj3, dt3, conv_w[0], conv_b, pad_h(dt_bias), pad_h(a_log), d_skip_ch, ssm_norm_g)

    out = _out_proj(x2, o_att.reshape(m, ATT_WIDTH), y_ssm.reshape(m, SSM_INNER), proj,
                    w_att_proj[0].astype(bf16), w_ssm_proj[0].astype(bf16), w_out[0].astype(bf16), g_post)
    return out.reshape(b, s, d)
```

```python
import functools
import math

import numpy as np
import jax
import jax.numpy as jnp
from jax import lax
from jax.experimental import pallas as pl
from jax.experimental.pallas import tpu as pltpu

D_MODEL = 1024
ATT_HEADS = 8
ATT_QK_DIM = 64
ATT_V_DIM = 2 * ATT_QK_DIM
ATT_WIDTH = ATT_HEADS * ATT_V_DIM
REL_BUCKETS = 32
REL_MAX_DIST = 128
SSM_INNER = 2 * D_MODEL
SSM_HEAD_DIM = 64
SSM_HEADS = SSM_INNER // SSM_HEAD_DIM
SSM_GROUPS = 8
SSM_HEADS_PER_GROUP = SSM_HEADS // SSM_GROUPS
SSM_STATE = 128
SSM_CONV = 4
SSM_CHUNK = 256
SSM_CONV_DIM = SSM_INNER + 2 * SSM_GROUPS * SSM_STATE
EPS = 1e-6
LAMBDA_INIT = 0.8 - 0.6 * math.exp(-0.3 * 0)

LANES = 128
SUBLANES = 8
VMEM_LIMIT_BYTES = 56 * 1024 * 1024
LOG2E = math.log2(math.e)
NEG_BIG = -1e30

COL_Q = 0
COL_K = COL_Q + ATT_WIDTH
COL_V = COL_K + ATT_WIDTH
COL_GATT = COL_V + ATT_WIDTH
COL_XBC = COL_GATT + ATT_WIDTH
COL_Z = COL_XBC + SSM_CONV_DIM
COL_GATE = COL_Z + SSM_INNER
PROJ_COLS = COL_GATE + 2 * D_MODEL

ATT_BLOCK = 512
N_BIAS_TILES = 3
ATT_SOFTMAX_ROWS = 64
ATT_UNROLL = 8
ATT_ACC_SLOTS = 4
PROJ_TM = 1024
PROJ_TN = 3072
OUT_TM = 1024


def _t5_bucket_thresholds():
    max_exact = REL_BUCKETS // 2
    n = np.arange(0, 4 * REL_MAX_DIST)
    nf = np.maximum(n, 1).astype(np.float32)
    large = max_exact + (np.log(nf / np.float32(max_exact)) / np.float32(math.log(REL_MAX_DIST / max_exact))
                         * np.float32(REL_BUCKETS - max_exact)).astype(np.int32)
    large = np.minimum(large, REL_BUCKETS - 1)
    bucket = np.where(n < max_exact, n, large)
    assert np.all(np.diff(bucket) >= 0)
    thr = [int(np.argmax(bucket >= b)) for b in range(REL_BUCKETS)]
    assert bucket[thr[-1]] == REL_BUCKETS - 1
    return thr


_BUCKET_THR = _t5_bucket_thresholds()


def _in_proj_kernel(x_ref, g_ref, w_ref, wdt_ref, scale_ref, o_ref, dt_ref, h_scr):
    @pl.when(pl.program_id(1) == 0)
    def _():
        xf = x_ref[...]
        ms = jnp.mean(xf * xf, axis=-1, keepdims=True)
        h = (xf * lax.rsqrt(ms + EPS) * g_ref[...]).astype(jnp.bfloat16)
        h_scr[...] = h
        dt_ref[...] = jnp.dot(h, wdt_ref[...], preferred_element_type=jnp.float32)

    acc = jnp.dot(h_scr[...], w_ref[...], preferred_element_type=jnp.float32)
    o_ref[...] = (acc * scale_ref[...]).astype(o_ref.dtype)


def _in_proj(x2, g_pre, w_main, w_dt, col_scale):
    m = x2.shape[0]
    tm = min(PROJ_TM, m)
    tn = PROJ_TN
    return pl.pallas_call(
        _in_proj_kernel,
        grid=(m // tm, PROJ_COLS // tn),
        in_specs=[
            pl.BlockSpec((tm, D_MODEL), lambda i, j: (i, 0)),
            pl.BlockSpec((1, D_MODEL), lambda i, j: (0, 0)),
            pl.BlockSpec((D_MODEL, tn), lambda i, j: (0, j)),
            pl.BlockSpec((D_MODEL, LANES), lambda i, j: (0, 0)),
            pl.BlockSpec((1, tn), lambda i, j: (0, j)),
        ],
        out_specs=[
            pl.BlockSpec((tm, tn), lambda i, j: (i, j)),
            pl.BlockSpec((tm, LANES), lambda i, j: (i, 0)),
        ],
        out_shape=[
            jax.ShapeDtypeStruct((m, PROJ_COLS), jnp.bfloat16),
            jax.ShapeDtypeStruct((m, LANES), jnp.float32),
        ],
        scratch_shapes=[pltpu.VMEM((tm, D_MODEL), jnp.bfloat16)],
        compiler_params=pltpu.CompilerParams(
            dimension_semantics=("arbitrary", "arbitrary"),
            vmem_limit_bytes=VMEM_LIMIT_BYTES),
        name="in_proj",
    )(x2, g_pre, w_main, w_dt, col_scale)


def _bias_kernel(rb_ref, o_ref, *, t):
    h = pl.program_id(0)
    row = lax.broadcasted_iota(jnp.int32, (t, t), 0)
    col = lax.broadcasted_iota(jnp.int32, (t, t), 1)
    far = rb_ref[REL_BUCKETS - 1, h]
    for d in range(N_BIAS_TILES):
        dist = d * t + row - col
        val = jnp.full((t, t), rb_ref[0, h], jnp.float32)
        for b in range(1, REL_BUCKETS):
            val = jnp.where(dist >= _BUCKET_THR[b], rb_ref[b, h], val)
        val = (val - far) * LOG2E
        if d == 0:
            val = jnp.where(dist >= 0, val, NEG_BIG)
        o_ref[d] = val


def _bias_tiles(rel_bias, t):
    assert t + 1 >= _BUCKET_THR[-1]
    return pl.pallas_call(
        functools.partial(_bias_kernel, t=t),
        grid=(ATT_HEADS,),
        in_specs=[pl.BlockSpec(memory_space=pltpu.SMEM)],
        out_specs=pl.BlockSpec((None, N_BIAS_TILES, t, t), lambda h: (h, 0, 0, 0)),
        out_shape=jax.ShapeDtypeStruct((ATT_HEADS, N_BIAS_TILES, t, t), jnp.float32),
        compiler_params=pltpu.CompilerParams(dimension_semantics=("arbitrary",)),
        name="t5_bias",
    )(rel_bias)


def _attn_kernel(q_ref, k_ref, v_ref, gatt_ref, bias_ref, subln_ref, lq1_ref, lk1_ref, lq2_ref, lk2_ref,
                 o_ref, vaug_scr, m_scr, acc_scr, s0_scr, s1_scr, p0_scr, p1_scr, al0_scr, al1_scr, *, t, nq):
    dv = ATT_V_DIM
    s_bufs, p_bufs, al_bufs = (s0_scr, s1_scr), (p0_scr, p1_scr), (al0_scr, al1_scr)
    steps = [(qi, j) for qi in range(nq) for j in range(qi + 1)]
    n_steps = len(steps)

    vaug_scr[:, :dv] = v_ref[...]
    vaug_scr[:, dv:] = jnp.ones((v_ref.shape[0], dv), v_ref.dtype)
    acc_scr[...] = jnp.zeros(acc_scr.shape, jnp.float32)
    m_scr[...] = jnp.full(m_scr.shape, -jnp.inf, jnp.float32)

    def rows(idx):
        if isinstance(idx, int):
            return pl.ds(idx * t, t)
        return pl.ds(pl.multiple_of(idx * t, t), t)

    def stage_a(qi, j, s_dst):
        q = q_ref[rows(qi), :]
        lane = lax.broadcasted_iota(jnp.int32, q.shape, 1)
        zero = jnp.zeros_like(q)
        qs = jnp.concatenate([jnp.where(lane < ATT_QK_DIM, q, zero),
                              jnp.where(lane >= ATT_QK_DIM, q, zero)], axis=0)
        d = min(qi - j, N_BIAS_TILES - 1) if isinstance(qi, int) else jnp.minimum(qi - j, N_BIAS_TILES - 1)
        s = lax.dot_general(qs, k_ref[rows(j), :], (((1,), (1,)), ((), ())),
                            preferred_element_type=jnp.float32)
        s_dst[...] = (s.reshape(2, t, t) + bias_ref[d][None]).reshape(2 * t, t)

    def stage_b(qi, j, s_src, p_dst, al_dst):
        for r0 in range(0, 2 * t, ATT_SOFTMAX_ROWS):
            rs = slice(r0, r0 + ATT_SOFTMAX_ROWS)
            m_old = m_scr[rs, :]
            if isinstance(j, int):
                m_prev = jnp.full(m_old.shape, -jnp.inf, jnp.float32) if j == 0 else m_old
            else:
                m_prev = jnp.where(j == 0, -jnp.inf, m_old)
            m_new = jnp.maximum(m_prev, jnp.max(s_src[rs, :], axis=-1, keepdims=True))
            al_dst[rs, :] = jnp.exp2(m_prev - m_new)
            m_scr[rs, :] = m_new
        for r0 in range(0, 2 * t, ATT_SOFTMAX_ROWS):
            rs = slice(r0, r0 + ATT_SOFTMAX_ROWS)
            p_dst[rs, :] = jnp.exp2(s_src[rs, :] - m_scr[rs, :][:, :1]).astype(jnp.bfloat16)

    def finalize(qi):
        acc = acc_scr[qi % ATT_ACC_SLOTS] if isinstance(qi, int) else acc_scr[qi & (ATT_ACC_SLOTS - 1)]
        o = acc[:, :dv] / acc[:, dv:]
        lam = (jnp.exp(jnp.sum(lq1_ref[...] * lk1_ref[...], axis=-1, keepdims=True))
               - jnp.exp(jnp.sum(lq2_ref[...] * lk2_ref[...], axis=-1, keepdims=True))
               + LAMBDA_INIT)
        o = o[:t] - lam * o[t:]
        ms = jnp.mean(o * o, axis=-1, keepdims=True)
        o = o * lax.rsqrt(ms + EPS) * subln_ref[...] * (1.0 - LAMBDA_INIT)
        g = gatt_ref[rows(qi), :].astype(jnp.float32)
        o = o * (g / (1.0 + jnp.exp(-g)))
        o_ref[rows(qi), :] = o.astype(o_ref.dtype)

    def stage_c(qi, j, p_src, al_src):
        slot = qi % ATT_ACC_SLOTS if isinstance(qi, int) else qi & (ATT_ACC_SLOTS - 1)
        pv = jnp.dot(p_src[...], vaug_scr[rows(j), :], preferred_element_type=jnp.float32)
        al = al_src[...]
        acc_scr[slot] = acc_scr[slot] * jnp.concatenate([al, al], axis=1) + pv

    def iteration(par, a, b, c):
        if a is not None:
            stage_a(a[0], a[1], s_bufs[par])
        if b is not None:
            stage_b(b[0], b[1], s_bufs[1 - par], p_bufs[1 - par], al_bufs[1 - par])
        if c is not None:
            stage_c(c[0], c[1], p_bufs[par], al_bufs[par])

    def static_iteration(it):
        pick = lambda i: steps[i] if 0 <= i < n_steps else None
        iteration(it % 2, pick(it), pick(it - 1), pick(it - 2))
        c = pick(it - 2)
        if c is not None and c[0] == c[1]:
            finalize(c[0])

    def advance(qi, j):
        wrap = j == qi
        return jnp.where(wrap, qi + 1, qi), jnp.where(wrap, 0, j + 1)

    unroll = ATT_UNROLL
    assert unroll % 2 == 0 and ATT_ACC_SLOTS == 4
    first_q = unroll // 2 - 1
    lo = 2 + first_q * (first_q + 1) // 2
    lo += lo % 2
    n_trips = max(n_steps - lo, 0) // unroll
    loop_end = lo + unroll * n_trips if n_trips > 0 else 0
    if n_trips > 0:
        for it in range(lo):
            static_iteration(it)

        def trip(_, carry):
            a, b, c = carry[0:2], carry[2:4], carry[4:6]
            done1, done2 = jnp.bool_(False), jnp.bool_(False)
            q1, q2 = jnp.int32(0), jnp.int32(0)
            for k in range(unroll):
                iteration(k % 2, a, b, c)
                last = c[0] == c[1]
                first, second = last & ~done1, last & done1
                q1, q2 = jnp.where(first, c[0], q1), jnp.where(second, c[0], q2)
                done1, done2 = done1 | last, done2 | second
                a, b, c = advance(*a), a, b
            pl.when(done1)(lambda: finalize(q1))
            pl.when(done2)(lambda: finalize(q2))
            return (*a, *b, *c)

        init = tuple(jnp.int32(v) for i in (lo, lo - 1, lo - 2) for v in steps[i])
        lax.fori_loop(0, n_trips, trip, init)
    for it in range(loop_end, n_steps + 2):
        static_iteration(it)


def _attention(proj3, bias, subln_g, lq1, lk1, lq2, lk2, t):
    b, s, _ = proj3.shape
    hd = ATT_V_DIM
    kq, kk, kv, kg = COL_Q // hd, COL_K // hd, COL_V // hd, COL_GATT // hd
    vec = lambda n: pl.BlockSpec((1, n), lambda bi, h: (0, 0))
    head_cols = lambda first: pl.BlockSpec((None, s, hd), lambda bi, h: (bi, 0, first + h))
    return pl.pallas_call(
        functools.partial(_attn_kernel, t=t, nq=s // t),
        grid=(b, ATT_HEADS),
        in_specs=[
            head_cols(kq), head_cols(kk), head_cols(kv), head_cols(kg),
            pl.BlockSpec((None, N_BIAS_TILES, t, t), lambda bi, h: (h, 0, 0, 0)),
            vec(hd), vec(ATT_QK_DIM), vec(ATT_QK_DIM), vec(ATT_QK_DIM), vec(ATT_QK_DIM),
        ],
        out_specs=head_cols(0),
        out_shape=jax.ShapeDtypeStruct((b, s, ATT_WIDTH), jnp.bfloat16),
        scratch_shapes=[
            pltpu.VMEM((s, 2 * hd), jnp.bfloat16),
            pltpu.VMEM((2 * t, LANES), jnp.float32),
            pltpu.VMEM((ATT_ACC_SLOTS, 2 * t, 2 * hd), jnp.float32),
            pltpu.VMEM((2 * t, t), jnp.float32), pltpu.VMEM((2 * t, t), jnp.float32),
            pltpu.VMEM((2 * t, t), jnp.bfloat16), pltpu.VMEM((2 * t, t), jnp.bfloat16),
            pltpu.VMEM((2 * t, LANES), jnp.float32), pltpu.VMEM((2 * t, LANES), jnp.float32),
        ],
        compiler_params=pltpu.CompilerParams(
            dimension_semantics=("arbitrary", "arbitrary"),
            vmem_limit_bytes=VMEM_LIMIT_BYTES),
        name="diff_attn",
    )(proj3, proj3, proj3, proj3, bias, subln_g, lq1, lk1, lq2, lk2)


def _silu(x):
    h = 0.5 * x
    return h + h * jnp.tanh(h)


def _ssd_kernel(xbc_ref, z_ref, dt_ref, shift_ref, convw_ref, convb_ref, dtb_ref, alog_ref, dskip_ref, normg_ref,
                o_ref, tail_scr, xs_scr, xsb_scr, b_scr, c_scr, y_scr, st_scr):
    c = pl.program_id(1)
    L = SSM_CHUNK
    P, N, G, HG = SSM_HEAD_DIM, SSM_STATE, SSM_GROUPS, SSM_HEADS_PER_GROUP
    halo = SUBLANES
    f32, bf16 = jnp.float32, jnp.bfloat16

    @pl.when(c == 0)
    def _():
        tail_scr[...] = jnp.zeros(tail_scr.shape, f32)
        st_scr[...] = jnp.zeros(st_scr.shape, f32)

    ct = 512
    for t0 in range(0, SSM_CONV_DIM, ct):
        cs = slice(t0, t0 + ct)
        w = [convw_ref[k:k + 1, cs] for k in range(SSM_CONV)]
        u = xbc_ref[:, cs].astype(f32)
        taps = jnp.concatenate([(u * w[SSM_CONV - 1 - k]).astype(bf16) for k in range(1, SSM_CONV)], axis=0)
        acc = (convb_ref[:, cs] + w[SSM_CONV - 1] * u
               + jnp.dot(shift_ref[...], taps, preferred_element_type=f32))
        corr = sum(w[SSM_CONV - 1 - k] * tail_scr[halo - k:2 * halo - k, cs] for k in range(1, SSM_CONV))
        act = _silu(jnp.concatenate([acc[:halo] + corr, acc[halo:]], axis=0))
        tail_scr[:halo, cs] = u[L - halo:]
        if t0 < SSM_INNER:
            xs_scr[:, cs] = act
            xsb_scr[:, cs] = act.astype(bf16)
        elif t0 < SSM_INNER + G * N:
            b_scr[:, t0 - SSM_INNER:t0 - SSM_INNER + ct] = act.astype(bf16)
        else:
            c_scr[:, t0 - SSM_INNER - G * N:t0 - SSM_INNER - G * N + ct] = act.astype(bf16)

    xdt = dt_ref[...] + dtb_ref[...]
    dtv = jnp.maximum(xdt, 0.0) + jnp.log1p(jnp.exp(-jnp.abs(xdt)))
    adt = dtv * (-jnp.exp(alog_ref[...]))
    row = lax.broadcasted_iota(jnp.int32, (L, L), 0)
    col = lax.broadcasted_iota(jnp.int32, (L, L), 1)
    tril = row >= col
    tri = jnp.where(tril, 1.0, 0.0).astype(bf16)
    hi = adt.astype(bf16)
    r1 = adt - hi.astype(f32)
    mid = r1.astype(bf16)
    lo = (r1 - mid.astype(f32)).astype(bf16)
    a_cs = (jnp.dot(tri, hi, preferred_element_type=f32)
            + jnp.dot(tri, mid, preferred_element_type=f32)
            + jnp.dot(tri, lo, preferred_element_type=f32))
    a_last = a_cs[L - 1:L, :]
    exp_last = jnp.exp(a_last)
    a2 = a_cs * LOG2E
    src2_t = (a2 - jnp.log2(dtv)).T
    w_end_t = (jnp.exp(a_last - a_cs) * dtv).T

    lane = lax.broadcasted_iota(jnp.int32, (L, LANES), 1)
    lane_row = lax.broadcasted_iota(jnp.int32, (1, LANES), 1)
    n_lt = L // LANES
    tril_blk = (lax.broadcasted_iota(jnp.int32, (LANES, LANES), 0)
                >= lax.broadcasted_iota(jnp.int32, (LANES, LANES), 1))

    def decay_block(cb, a_bc2, src_row, bi, bj):
        if bj > bi:
            return jnp.zeros((LANES, LANES), bf16)
        rs, cs = slice(bi * LANES, (bi + 1) * LANES), slice(bj * LANES, (bj + 1) * LANES)
        seg = a_bc2[rs, :] - src_row[:, cs]
        if bj == bi:
            seg = jnp.where(tril_blk, seg, -jnp.inf)
        return (cb[rs, cs] * jnp.exp2(seg)).astype(bf16)

    for g in range(G):
        b_g = b_scr[:, g * N:(g + 1) * N]
        c_g = c_scr[:, g * N:(g + 1) * N]
        cb = lax.dot_general(c_g, b_g, (((1,), (1,)), ((), ())), preferred_element_type=f32)
        b_g_t = b_g.astype(f32).T
        st_g = st_scr[g]
        y_off = jnp.dot(c_g, st_g.astype(bf16), preferred_element_type=f32)
        for pr in range(HG // 2):
            h0 = g * HG + 2 * pr
            cols = slice(h0 * P, (h0 + 2) * P)
            lcols = slice(2 * pr * P, (2 * pr + 2) * P)
            m_parts, bt_parts, ea_parts = [], [], []
            for h in (h0, h0 + 1):
                a_bc2 = jnp.broadcast_to(a2[:, h:h + 1], (L, LANES))
                src_row = src2_t[h:h + 1, :]
                m_parts.append(jnp.concatenate(
                    [jnp.concatenate([decay_block(cb, a_bc2, src_row, bi, bj) for bj in range(n_lt)], axis=1)
                     for bi in range(n_lt)], axis=0))
                bt_parts.append((b_g_t * w_end_t[h:h + 1, :]).astype(bf16))
                ea_parts.append(jnp.exp2(a_bc2))
            xsb = xsb_scr[:, cols]
            zero = jnp.zeros_like(xsb)
            lane_b = lax.broadcasted_iota(jnp.int32, xsb.shape, 1)
            x_bd = jnp.concatenate([jnp.where(lane_b < P, xsb, zero),
                                    jnp.where(lane_b >= P, xsb, zero)], axis=0)
            y_pair = jnp.dot(jnp.concatenate(m_parts, axis=1), x_bd, preferred_element_type=f32)
            ea = jnp.where(lane < P, ea_parts[0], ea_parts[1])
            y_scr[:, cols] = y_pair + y_off[:, lcols] * ea + xs_scr[:, cols] * dskip_ref[:, cols]
            upd = jnp.dot(jnp.concatenate(bt_parts, axis=1), x_bd, preferred_element_type=f32)
            el = jnp.where(lane_row < P, exp_last[:, h0:h0 + 1], exp_last[:, h0 + 1:h0 + 2])
            st_scr[g, :, lcols] = st_g[:, lcols] * el + upd

    gw = SSM_INNER // G
    for g in range(G):
        gs = slice(g * gw, (g + 1) * gw)
        zz = z_ref[:, gs].astype(f32)
        y = y_scr[:, gs] * _silu(zz)
        ms = jnp.mean(y * y, axis=-1, keepdims=True)
        o_ref[:, gs] = (y * lax.rsqrt(ms + EPS) * normg_ref[:, gs]).astype(o_ref.dtype)


def _ssd(proj3, dt3, conv_w, conv_b, dt_bias, a_log, d_skip_ch, norm_g):
    b, s, _ = proj3.shape
    L = SSM_CHUNK
    assert s % L == 0
    full = lambda shape: pl.BlockSpec(shape, lambda bi, c: (0,) * len(shape))
    t_idx = np.arange(L)
    shift = np.concatenate([(t_idx[None, :] == t_idx[:, None] - k) for k in range(1, SSM_CONV)], axis=1)
    shift = jnp.asarray(shift, jnp.bfloat16)
    return pl.pallas_call(
        _ssd_kernel,
        grid=(b, s // L),
        in_specs=[
            pl.BlockSpec((None, L, SSM_CONV_DIM), lambda bi, c: (bi, c, COL_XBC // SSM_CONV_DIM)),
            pl.BlockSpec((None, L, SSM_INNER), lambda bi, c: (bi, c, COL_Z // SSM_INNER)),
            pl.BlockSpec((None, L, LANES), lambda bi, c: (bi, c, 0)),
            full((L, (SSM_CONV - 1) * L)),
            full((SSM_CONV, SSM_CONV_DIM)), full((1, SSM_CONV_DIM)),
            full((1, LANES)), full((1, LANES)), full((1, SSM_INNER)), full((1, SSM_INNER)),
        ],
        out_specs=pl.BlockSpec((None, L, SSM_INNER), lambda bi, c: (bi, c, 0)),
        out_shape=jax.ShapeDtypeStruct((b, s, SSM_INNER), jnp.bfloat16),
        scratch_shapes=[
            pltpu.VMEM((2 * SUBLANES, SSM_CONV_DIM), jnp.float32),
            pltpu.VMEM((L, SSM_INNER), jnp.float32),
            pltpu.VMEM((L, SSM_INNER), jnp.bfloat16),
            pltpu.VMEM((L, SSM_GROUPS * SSM_STATE), jnp.bfloat16),
            pltpu.VMEM((L, SSM_GROUPS * SSM_STATE), jnp.bfloat16),
            pltpu.VMEM((L, SSM_INNER), jnp.float32),
            pltpu.VMEM((SSM_GROUPS, SSM_STATE, SSM_HEADS_PER_GROUP * SSM_HEAD_DIM), jnp.float32),
        ],
        compiler_params=pltpu.CompilerParams(
            dimension_semantics=("arbitrary", "arbitrary"),
            vmem_limit_bytes=VMEM_LIMIT_BYTES),
        name="ssd",
    )(proj3, proj3, dt3, shift, conv_w, conv_b, dt_bias, a_log, d_skip_ch, norm_g)


def _out_kernel(x_ref, oatt_ref, yssm_ref, ga_ref, gs_ref, watt_ref, wssm_ref, wout_ref, gpost_ref, o_ref):
    y_att = jnp.dot(oatt_ref[...], watt_ref[...], preferred_element_type=jnp.float32)
    y_ssm = jnp.dot(yssm_ref[...], wssm_ref[...], preferred_element_type=jnp.float32)
    ga = ga_ref[...].astype(jnp.float32)
    gs = gs_ref[...].astype(jnp.float32)
    mixed = y_att / (1.0 + jnp.exp(-ga)) + y_ssm / (1.0 + jnp.exp(-gs))
    out = jnp.dot(mixed.astype(jnp.bfloat16), wout_ref[...], preferred_element_type=jnp.float32)
    ms = jnp.mean(out * out, axis=-1, keepdims=True)
    o_ref[...] = x_ref[...] + out * lax.rsqrt(ms + EPS) * gpost_ref[...]


def _out_proj(x2, o_att, y_ssm, proj2, w_att, w_ssm, w_out, g_post):
    m = x2.shape[0]
    tm = min(OUT_TM, m)
    const = lambda shape: pl.BlockSpec(shape, lambda i: (0, 0), pipeline_mode=pl.Buffered(1))
    return pl.pallas_call(
        _out_kernel,
        grid=(m // tm,),
        in_specs=[
            pl.BlockSpec((tm, D_MODEL), lambda i: (i, 0)),
            pl.BlockSpec((tm, ATT_WIDTH), lambda i: (i, 0)),
            pl.BlockSpec((tm, SSM_INNER), lambda i: (i, 0)),
            pl.BlockSpec((tm, D_MODEL), lambda i: (i, COL_GATE // D_MODEL)),
            pl.BlockSpec((tm, D_MODEL), lambda i: (i, COL_GATE // D_MODEL + 1)),
            const((ATT_WIDTH, D_MODEL)), const((SSM_INNER, D_MODEL)), const((D_MODEL, D_MODEL)),
            const((1, D_MODEL)),
        ],
        out_specs=pl.BlockSpec((tm, D_MODEL), lambda i: (i, 0)),
        out_shape=jax.ShapeDtypeStruct((m, D_MODEL), jnp.float32),
        compiler_params=pltpu.CompilerParams(
            dimension_semantics=("arbitrary",),
            vmem_limit_bytes=VMEM_LIMIT_BYTES),
        name="out_proj",
    )(x2, o_att, y_ssm, proj2, proj2, w_att, w_ssm, w_out, g_post)


def kernel(x, g_pre, w_in, att_lambda_q1, att_lambda_k1, att_lambda_q2, att_lambda_k2, att_subln_g, rel_bias,
           conv_w, conv_b, dt_bias, a_log, d_skip, ssm_norm_g, w_att_proj, w_ssm_proj, w_out, g_post):
    b, s, d = x.shape
    assert d == D_MODEL and g_pre.shape[0] == 1
    m = b * s
    f32, bf16 = jnp.float32, jnp.bfloat16
    t = min(ATT_BLOCK, s)
    assert s % t == 0 and s % SSM_CHUNK == 0

    w = w_in[0]
    o_q, o_k, o_v, o_g = 0, ATT_WIDTH, 2 * ATT_WIDTH, 3 * ATT_WIDTH
    o_z = 4 * ATT_WIDTH
    o_xbc = o_z + SSM_INNER
    o_dt = o_xbc + SSM_CONV_DIM
    o_gate = o_dt + SSM_HEADS
    w_main = jnp.concatenate([w[:, o_q:o_z], w[:, o_xbc:o_dt], w[:, o_z:o_xbc], w[:, o_gate:]], axis=1).astype(bf16)
    w_dt = jnp.pad(w[:, o_dt:o_gate], ((0, 0), (0, LANES - SSM_HEADS))).astype(bf16)
    col_scale = jnp.concatenate([jnp.full((1, ATT_WIDTH), ATT_QK_DIM ** -0.5 * LOG2E, f32),
                                 jnp.ones((1, PROJ_COLS - ATT_WIDTH), f32)], axis=1)

    x2 = x.reshape(m, d)
    proj, dt_raw = _in_proj(x2, g_pre, w_main, w_dt, col_scale)
    proj3 = proj.reshape(b, s, PROJ_COLS)
    dt3 = dt_raw.reshape(b, s, LANES)

    bias = _bias_tiles(rel_bias, t)
    o_att = _attention(proj3, bias, att_subln_g, att_lambda_q1, att_lambda_k1, att_lambda_q2, att_lambda_k2, t)

    pad_h = lambda v: jnp.pad(v, ((0, 0), (0, LANES - SSM_HEADS)))
    d_skip_ch = jnp.repeat(d_skip, SSM_HEAD_DIM, axis=1)
    y_ssm = _ssd(proj3, dt3, conv_w[0], conv_b, pad_h(dt_bias), pad_h(a_log), d_skip_ch, ssm_norm_g)

    out = _out_proj(x2, o_att.reshape(m, ATT_WIDTH), y_ssm.reshape(m, SSM_INNER), proj,
                    w_att_proj[0].astype(bf16), w_ssm_proj[0].astype(bf16), w_out[0].astype(bf16), g_post)
    return out.reshape(b, s, d)
```

```python
import functools
import math

import numpy as np
import jax
import jax.numpy as jnp
from jax import lax
from jax.experimental import pallas as pl
from jax.experimental.pallas import tpu as pltpu

D_MODEL = 1024
ATT_HEADS = 8
ATT_QK_DIM = 64
ATT_V_DIM = 2 * ATT_QK_DIM
ATT_WIDTH = ATT_HEADS * ATT_V_DIM
REL_BUCKETS = 32
REL_MAX_DIST = 128
SSM_INNER = 2 * D_MODEL
SSM_HEAD_DIM = 64
SSM_HEADS = SSM_INNER // SSM_HEAD_DIM
SSM_GROUPS = 8
SSM_HEADS_PER_GROUP = SSM_HEADS // SSM_GROUPS
SSM_STATE = 128
SSM_CONV = 4
SSM_CHUNK = 256
SSM_CONV_DIM = SSM_INNER + 2 * SSM_GROUPS * SSM_STATE
EPS = 1e-6
LAMBDA_INIT = 0.8 - 0.6 * math.exp(-0.3 * 0)

LANES = 128
SUBLANES = 8
VMEM_LIMIT_BYTES = 56 * 1024 * 1024
LOG2E = math.log2(math.e)
NEG_BIG = -1e30

COL_Q = 0
COL_K = COL_Q + ATT_WIDTH
COL_V = COL_K + ATT_WIDTH
COL_GATT = COL_V + ATT_WIDTH
COL_XBC = COL_GATT + ATT_WIDTH
COL_Z = COL_XBC + SSM_CONV_DIM
COL_GATE = COL_Z + SSM_INNER
PROJ_COLS = COL_GATE + 2 * D_MODEL

ATT_BLOCK = 512
N_BIAS_TILES = 3
ATT_SOFTMAX_ROWS = 64
ATT_UNROLL = 8
ATT_ACC_SLOTS = 4
PROJ_TM = 1024
PROJ_TN = 3072
OUT_TM = 1024
SSD_CHUNKS_PER_STEP = 2


def _t5_bucket_thresholds():
    max_exact = REL_BUCKETS // 2
    n = np.arange(0, 4 * REL_MAX_DIST)
    nf = np.maximum(n, 1).astype(np.float32)
    large = max_exact + (np.log(nf / np.float32(max_exact)) / np.float32(math.log(REL_MAX_DIST / max_exact))
                         * np.float32(REL_BUCKETS - max_exact)).astype(np.int32)
    large = np.minimum(large, REL_BUCKETS - 1)
    bucket = np.where(n < max_exact, n, large)
    assert np.all(np.diff(bucket) >= 0)
    thr = [int(np.argmax(bucket >= b)) for b in range(REL_BUCKETS)]
    assert bucket[thr[-1]] == REL_BUCKETS - 1
    return thr


_BUCKET_THR = _t5_bucket_thresholds()


def _in_proj_kernel(x_ref, g_ref, w_ref, wdt_ref, scale_ref, o_ref, dt_ref, h_scr):
    @pl.when(pl.program_id(1) == 0)
    def _():
        xf = x_ref[...]
        ms = jnp.mean(xf * xf, axis=-1, keepdims=True)
        h = (xf * lax.rsqrt(ms + EPS) * g_ref[...]).astype(jnp.bfloat16)
        h_scr[...] = h
        dt_ref[...] = jnp.dot(h, wdt_ref[...], preferred_element_type=jnp.float32)

    acc = jnp.dot(h_scr[...], w_ref[...], preferred_element_type=jnp.float32)
    o_ref[...] = (acc * scale_ref[...]).astype(o_ref.dtype)


def _in_proj(x2, g_pre, w_main, w_dt, col_scale):
    m = x2.shape[0]
    tm = min(PROJ_TM, m)
    tn = PROJ_TN
    return pl.pallas_call(
        _in_proj_kernel,
        grid=(m // tm, PROJ_COLS // tn),
        in_specs=[
            pl.BlockSpec((tm, D_MODEL), lambda i, j: (i, 0)),
            pl.BlockSpec((1, D_MODEL), lambda i, j: (0, 0)),
            pl.BlockSpec((D_MODEL, tn), lambda i, j: (0, j)),
            pl.BlockSpec((D_MODEL, LANES), lambda i, j: (0, 0)),
            pl.BlockSpec((1, tn), lambda i, j: (0, j)),
        ],
        out_specs=[
            pl.BlockSpec((tm, tn), lambda i, j: (i, j)),
            pl.BlockSpec((tm, LANES), lambda i, j: (i, 0)),
        ],
        out_shape=[
            jax.ShapeDtypeStruct((m, PROJ_COLS), jnp.bfloat16),
            jax.ShapeDtypeStruct((m, LANES), jnp.float32),
        ],
        scratch_shapes=[pltpu.VMEM((tm, D_MODEL), jnp.bfloat16)],
        compiler_params=pltpu.CompilerParams(
            dimension_semantics=("arbitrary", "arbitrary"),
            vmem_limit_bytes=VMEM_LIMIT_BYTES),
        name="in_proj",
    )(x2, g_pre, w_main, w_dt, col_scale)


def _bias_kernel(rb_ref, o_ref, *, t):
    h = pl.program_id(0)
    row = lax.broadcasted_iota(jnp.int32, (t, t), 0)
    col = lax.broadcasted_iota(jnp.int32, (t, t), 1)
    far = rb_ref[REL_BUCKETS - 1, h]
    for d in range(N_BIAS_TILES):
        dist = d * t + row - col
        val = jnp.full((t, t), rb_ref[0, h], jnp.float32)
        for b in range(1, REL_BUCKETS):
            val = jnp.where(dist >= _BUCKET_THR[b], rb_ref[b, h], val)
        val = (val - far) * LOG2E
        if d == 0:
            val = jnp.where(dist >= 0, val, NEG_BIG)
        o_ref[d] = val


def _bias_tiles(rel_bias, t):
    assert t + 1 >= _BUCKET_THR[-1]
    return pl.pallas_call(
        functools.partial(_bias_kernel, t=t),
        grid=(ATT_HEADS,),
        in_specs=[pl.BlockSpec(memory_space=pltpu.SMEM)],
        out_specs=pl.BlockSpec((None, N_BIAS_TILES, t, t), lambda h: (h, 0, 0, 0)),
        out_shape=jax.ShapeDtypeStruct((ATT_HEADS, N_BIAS_TILES, t, t), jnp.float32),
        compiler_params=pltpu.CompilerParams(dimension_semantics=("arbitrary",)),
        name="t5_bias",
    )(rel_bias)


def _attn_kernel(q_ref, k_ref, v_ref, gatt_ref, bias_ref, subln_ref, lq1_ref, lk1_ref, lq2_ref, lk2_ref,
                 o_ref, vaug_scr, m_scr, acc_scr, s0_scr, s1_scr, p0_scr, p1_scr, al0_scr, al1_scr, *, t, nq):
    dv = ATT_V_DIM
    s_bufs, p_bufs, al_bufs = (s0_scr, s1_scr), (p0_scr, p1_scr), (al0_scr, al1_scr)
    steps = [(qi, j) for qi in range(nq) for j in range(qi + 1)]
    n_steps = len(steps)

    vaug_scr[:, :dv] = v_ref[...]
    vaug_scr[:, dv:] = jnp.ones((v_ref.shape[0], dv), v_ref.dtype)
    acc_scr[...] = jnp.zeros(acc_scr.shape, jnp.float32)
    m_scr[...] = jnp.full(m_scr.shape, -jnp.inf, jnp.float32)

    def rows(idx):
        if isinstance(idx, int):
            return pl.ds(idx * t, t)
        return pl.ds(pl.multiple_of(idx * t, t), t)

    def stage_a(qi, j, s_dst):
        q = q_ref[rows(qi), :]
        lane = lax.broadcasted_iota(jnp.int32, q.shape, 1)
        zero = jnp.zeros_like(q)
        qs = jnp.concatenate([jnp.where(lane < ATT_QK_DIM, q, zero),
                              jnp.where(lane >= ATT_QK_DIM, q, zero)], axis=0)
        d = min(qi - j, N_BIAS_TILES - 1) if isinstance(qi, int) else jnp.minimum(qi - j, N_BIAS_TILES - 1)
        s = lax.dot_general(qs, k_ref[rows(j), :], (((1,), (1,)), ((), ())),
                            preferred_element_type=jnp.float32)
        s_dst[...] = (s.reshape(2, t, t) + bias_ref[d][None]).reshape(2 * t, t)

    def stage_b(qi, j, s_src, p_dst, al_dst):
        for r0 in range(0, 2 * t, ATT_SOFTMAX_ROWS):
            rs = slice(r0, r0 + ATT_SOFTMAX_ROWS)
            m_old = m_scr[rs, :]
            if isinstance(j, int):
                m_prev = jnp.full(m_old.shape, -jnp.inf, jnp.float32) if j == 0 else m_old
            else:
                m_prev = jnp.where(j == 0, -jnp.inf, m_old)
            m_new = jnp.maximum(m_prev, jnp.max(s_src[rs, :], axis=-1, keepdims=True))
            al_dst[rs, :] = jnp.exp2(m_prev - m_new)
            m_scr[rs, :] = m_new
        for r0 in range(0, 2 * t, ATT_SOFTMAX_ROWS):
            rs = slice(r0, r0 + ATT_SOFTMAX_ROWS)
            p_dst[rs, :] = jnp.exp2(s_src[rs, :] - m_scr[rs, :][:, :1]).astype(jnp.bfloat16)

    def finalize(qi):
        acc = acc_scr[qi % ATT_ACC_SLOTS] if isinstance(qi, int) else acc_scr[qi & (ATT_ACC_SLOTS - 1)]
        o = acc[:, :dv] / acc[:, dv:]
        lam = (jnp.exp(jnp.sum(lq1_ref[...] * lk1_ref[...], axis=-1, keepdims=True))
               - jnp.exp(jnp.sum(lq2_ref[...] * lk2_ref[...], axis=-1, keepdims=True))
               + LAMBDA_INIT)
        o = o[:t] - lam * o[t:]
        ms = jnp.mean(o * o, axis=-1, keepdims=True)
        o = o * lax.rsqrt(ms + EPS) * subln_ref[...] * (1.0 - LAMBDA_INIT)
        g = gatt_ref[rows(qi), :].astype(jnp.float32)
        o = o * (g / (1.0 + jnp.exp(-g)))
        o_ref[rows(qi), :] = o.astype(o_ref.dtype)

    def stage_c(qi, j, p_src, al_src):
        slot = qi % ATT_ACC_SLOTS if isinstance(qi, int) else qi & (ATT_ACC_SLOTS - 1)
        pv = jnp.dot(p_src[...], vaug_scr[rows(j), :], preferred_element_type=jnp.float32)
        al = al_src[...]
        acc_scr[slot] = acc_scr[slot] * jnp.concatenate([al, al], axis=1) + pv

    def iteration(par, a, b, c):
        if a is not None:
            stage_a(a[0], a[1], s_bufs[par])
        if b is not None:
            stage_b(b[0], b[1], s_bufs[1 - par], p_bufs[1 - par], al_bufs[1 - par])
        if c is not None:
            stage_c(c[0], c[1], p_bufs[par], al_bufs[par])

    def static_iteration(it):
        pick = lambda i: steps[i] if 0 <= i < n_steps else None
        iteration(it % 2, pick(it), pick(it - 1), pick(it - 2))
        c = pick(it - 2)
        if c is not None and c[0] == c[1]:
            finalize(c[0])

    def advance(qi, j):
        wrap = j == qi
        return jnp.where(wrap, qi + 1, qi), jnp.where(wrap, 0, j + 1)

    unroll = ATT_UNROLL
    assert unroll % 2 == 0 and ATT_ACC_SLOTS == 4
    first_q = unroll // 2 - 1
    lo = 2 + first_q * (first_q + 1) // 2
    lo += lo % 2
    n_trips = max(n_steps - lo, 0) // unroll
    loop_end = lo + unroll * n_trips if n_trips > 0 else 0
    if n_trips > 0:
        for it in range(lo):
            static_iteration(it)

        def trip(_, carry):
            a, b, c = carry[0:2], carry[2:4], carry[4:6]
            done1, done2 = jnp.bool_(False), jnp.bool_(False)
            q1, q2 = jnp.int32(0), jnp.int32(0)
            for k in range(unroll):
                iteration(k % 2, a, b, c)
                last = c[0] == c[1]
                first, second = last & ~done1, last & done1
                q1, q2 = jnp.where(first, c[0], q1), jnp.where(second, c[0], q2)
                done1, done2 = done1 | last, done2 | second
                a, b, c = advance(*a), a, b
            pl.when(done1)(lambda: finalize(q1))
            pl.when(done2)(lambda: finalize(q2))
            return (*a, *b, *c)

        init = tuple(jnp.int32(v) for i in (lo, lo - 1, lo - 2) for v in steps[i])
        lax.fori_loop(0, n_trips, trip, init)
    for it in range(loop_end, n_steps + 2):
        static_iteration(it)


def _attention(proj3, bias, subln_g, lq1, lk1, lq2, lk2, t):
    b, s, _ = proj3.shape
    hd = ATT_V_DIM
    kq, kk, kv, kg = COL_Q // hd, COL_K // hd, COL_V // hd, COL_GATT // hd
    vec = lambda n: pl.BlockSpec((1, n), lambda bi, h: (0, 0))
    head_cols = lambda first: pl.BlockSpec((None, s, hd), lambda bi, h: (bi, 0, first + h))
    return pl.pallas_call(
        functools.partial(_attn_kernel, t=t, nq=s // t),
        grid=(b, ATT_HEADS),
        in_specs=[
            head_cols(kq), head_cols(kk), head_cols(kv), head_cols(kg),
            pl.BlockSpec((None, N_BIAS_TILES, t, t), lambda bi, h: (h, 0, 0, 0)),
            vec(hd), vec(ATT_QK_DIM), vec(ATT_QK_DIM), vec(ATT_QK_DIM), vec(ATT_QK_DIM),
        ],
        out_specs=head_cols(0),
        out_shape=jax.ShapeDtypeStruct((b, s, ATT_WIDTH), jnp.bfloat16),
        scratch_shapes=[
            pltpu.VMEM((s, 2 * hd), jnp.bfloat16),
            pltpu.VMEM((2 * t, LANES), jnp.float32),
            pltpu.VMEM((ATT_ACC_SLOTS, 2 * t, 2 * hd), jnp.float32),
            pltpu.VMEM((2 * t, t), jnp.float32), pltpu.VMEM((2 * t, t), jnp.float32),
            pltpu.VMEM((2 * t, t), jnp.bfloat16), pltpu.VMEM((2 * t, t), jnp.bfloat16),
            pltpu.VMEM((2 * t, LANES), jnp.float32), pltpu.VMEM((2 * t, LANES), jnp.float32),
        ],
        compiler_params=pltpu.CompilerParams(
            dimension_semantics=("arbitrary", "arbitrary"),
            vmem_limit_bytes=VMEM_LIMIT_BYTES),
        name="diff_attn",
    )(proj3, proj3, proj3, proj3, bias, subln_g, lq1, lk1, lq2, lk2)


def _silu(x):
    h = 0.5 * x
    return h + h * jnp.tanh(h)


def _ssd_kernel(xbc_ref, z_ref, dt_ref, shift_ref, convw_ref, convb_ref, dtb_ref, alog_ref, dskip_ref, normg_ref,
                o_ref, tail_scr, xs_scr, xsb_scr, b_scr, c_scr, y_scr, st_scr):
    @pl.when(pl.program_id(1) == 0)
    def _():
        tail_scr[...] = jnp.zeros(tail_scr.shape, jnp.float32)
        st_scr[...] = jnp.zeros(st_scr.shape, jnp.float32)

    def chunk(ci, carry):
        rows = pl.ds(pl.multiple_of(ci * SSM_CHUNK, SSM_CHUNK), SSM_CHUNK)
        _ssd_chunk(rows, xbc_ref, z_ref, dt_ref, shift_ref, convw_ref, convb_ref, dtb_ref, alog_ref, dskip_ref,
                   normg_ref, o_ref, tail_scr, xs_scr, xsb_scr, b_scr, c_scr, y_scr, st_scr)
        return carry

    lax.fori_loop(0, SSD_CHUNKS_PER_STEP, chunk, 0)


def _ssd_chunk(rows, xbc_ref, z_ref, dt_ref, shift_ref, convw_ref, convb_ref, dtb_ref, alog_ref, dskip_ref,
               normg_ref, o_ref, tail_scr, xs_scr, xsb_scr, b_scr, c_scr, y_scr, st_scr):
    L = SSM_CHUNK
    P, N, G, HG = SSM_HEAD_DIM, SSM_STATE, SSM_GROUPS, SSM_HEADS_PER_GROUP
    halo = SUBLANES
    f32, bf16 = jnp.float32, jnp.bfloat16

    ct = 512
    for t0 in range(0, SSM_CONV_DIM, ct):
        cs = slice(t0, t0 + ct)
        w = [convw_ref[k:k + 1, cs] for k in range(SSM_CONV)]
        u = xbc_ref[rows, cs].astype(f32)
        taps = jnp.concatenate([(u * w[SSM_CONV - 1 - k]).astype(bf16) for k in range(1, SSM_CONV)], axis=0)
        acc = (convb_ref[:, cs] + w[SSM_CONV - 1] * u
               + jnp.dot(shift_ref[...], taps, preferred_element_type=f32))
        corr = sum(w[SSM_CONV - 1 - k] * tail_scr[halo - k:2 * halo - k, cs] for k in range(1, SSM_CONV))
        act = _silu(jnp.concatenate([acc[:halo] + corr, acc[halo:]], axis=0))
        tail_scr[:halo, cs] = u[L - halo:]
        if t0 < SSM_INNER:
            xs_scr[:, cs] = act
            xsb_scr[:, cs] = act.astype(bf16)
        elif t0 < SSM_INNER + G * N:
            b_scr[:, t0 - SSM_INNER:t0 - SSM_INNER + ct] = act.astype(bf16)
        else:
            c_scr[:, t0 - SSM_INNER - G * N:t0 - SSM_INNER - G * N + ct] = act.astype(bf16)

    xdt = dt_ref[rows, :] + dtb_ref[...]
    dtv = jnp.maximum(xdt, 0.0) + jnp.log1p(jnp.exp(-jnp.abs(xdt)))
    adt = dtv * (-jnp.exp(alog_ref[...]))
    row = lax.broadcasted_iota(jnp.int32, (L, L), 0)
    col = lax.broadcasted_iota(jnp.int32, (L, L), 1)
    tril = row >= col
    tri = jnp.where(tril, 1.0, 0.0).astype(bf16)
    hi = adt.astype(bf16)
    r1 = adt - hi.astype(f32)
    mid = r1.astype(bf16)
    lo = (r1 - mid.astype(f32)).astype(bf16)
    a_cs = (jnp.dot(tri, hi, preferred_element_type=f32)
            + jnp.dot(tri, mid, preferred_element_type=f32)
            + jnp.dot(tri, lo, preferred_element_type=f32))
    a_last = a_cs[L - 1:L, :]
    exp_last = jnp.exp(a_last)
    a2 = a_cs * LOG2E
    src2_t = (a2 - jnp.log2(dtv)).T
    w_end_t = (jnp.exp(a_last - a_cs) * dtv).T

    lane = lax.broadcasted_iota(jnp.int32, (L, LANES), 1)
    lane_row = lax.broadcasted_iota(jnp.int32, (1, LANES), 1)
    n_lt = L // LANES
    tril_blk = (lax.broadcasted_iota(jnp.int32, (LANES, LANES), 0)
                >= lax.broadcasted_iota(jnp.int32, (LANES, LANES), 1))

    def decay_block(cb, a_bc2, src_row, bi, bj):
        if bj > bi:
            return jnp.zeros((LANES, LANES), bf16)
        rs, cs = slice(bi * LANES, (bi + 1) * LANES), slice(bj * LANES, (bj + 1) * LANES)
        seg = a_bc2[rs, :] - src_row[:, cs]
        if bj == bi:
            seg = jnp.where(tril_blk, seg, -jnp.inf)
        return (cb[rs, cs] * jnp.exp2(seg)).astype(bf16)

    for g in range(G):
        b_g = b_scr[:, g * N:(g + 1) * N]
        c_g = c_scr[:, g * N:(g + 1) * N]
        cb = lax.dot_general(c_g, b_g, (((1,), (1,)), ((), ())), preferred_element_type=f32)
        b_g_t = b_g.astype(f32).T
        st_g = st_scr[g]
        y_off = jnp.dot(c_g, st_g.astype(bf16), preferred_element_type=f32)
        for pr in range(HG // 2):
            h0 = g * HG + 2 * pr
            cols = slice(h0 * P, (h0 + 2) * P)
            lcols = slice(2 * pr * P, (2 * pr + 2) * P)
            m_parts, bt_parts, ea_parts = [], [], []
            for h in (h0, h0 + 1):
                a_bc2 = jnp.broadcast_to(a2[:, h:h + 1], (L, LANES))
                src_row = src2_t[h:h + 1, :]
                m_parts.append(jnp.concatenate(
                    [jnp.concatenate([decay_block(cb, a_bc2, src_row, bi, bj) for bj in range(n_lt)], axis=1)
                     for bi in range(n_lt)], axis=0))
                bt_parts.append((b_g_t * w_end_t[h:h + 1, :]).astype(bf16))
                ea_parts.append(jnp.exp2(a_bc2))
            xsb = xsb_scr[:, cols]
            zero = jnp.zeros_like(xsb)
            lane_b = lax.broadcasted_iota(jnp.int32, xsb.shape, 1)
            x_bd = jnp.concatenate([jnp.where(lane_b < P, xsb, zero),
                                    jnp.where(lane_b >= P, xsb, zero)], axis=0)
            y_pair = jnp.dot(jnp.concatenate(m_parts, axis=1), x_bd, preferred_element_type=f32)
            ea = jnp.where(lane < P, ea_parts[0], ea_parts[1])
            y_scr[:, cols] = y_pair + y_off[:, lcols] * ea + xs_scr[:, cols] * dskip_ref[:, cols]
            upd = jnp.dot(jnp.concatenate(bt_parts, axis=1), x_bd, preferred_element_type=f32)
            el = jnp.where(lane_row < P, exp_last[:, h0:h0 + 1], exp_last[:, h0 + 1:h0 + 2])
            st_scr[g, :, lcols] = st_g[:, lcols] * el + upd

    gw = SSM_INNER // G
    for g in range(G):
        gs = slice(g * gw, (g + 1) * gw)
        zz = z_ref[rows, gs].astype(f32)
        y = y_scr[:, gs] * _silu(zz)
        ms = jnp.mean(y * y, axis=-1, keepdims=True)
        o_ref[rows, gs] = (y * lax.rsqrt(ms + EPS) * normg_ref[:, gs]).astype(o_ref.dtype)


def _ssd(proj3, dt3, conv_w, conv_b, dt_bias, a_log, d_skip_ch, norm_g):
    b, s, _ = proj3.shape
    L = SSM_CHUNK
    assert s % L == 0
    full = lambda shape: pl.BlockSpec(shape, lambda bi, c: (0,) * len(shape))
    t_idx = np.arange(L)
    shift = np.concatenate([(t_idx[None, :] == t_idx[:, None] - k) for k in range(1, SSM_CONV)], axis=1)
    shift = jnp.asarray(shift, jnp.bfloat16)
    rows = L * SSD_CHUNKS_PER_STEP
    assert s % rows == 0
    return pl.pallas_call(
        _ssd_kernel,
        grid=(b, s // rows),
        in_specs=[
            pl.BlockSpec((None, rows, SSM_CONV_DIM), lambda bi, c: (bi, c, COL_XBC // SSM_CONV_DIM)),
            pl.BlockSpec((None, rows, SSM_INNER), lambda bi, c: (bi, c, COL_Z // SSM_INNER)),
            pl.BlockSpec((None, rows, LANES), lambda bi, c: (bi, c, 0)),
            full((L, (SSM_CONV - 1) * L)),
            full((SSM_CONV, SSM_CONV_DIM)), full((1, SSM_CONV_DIM)),
            full((1, LANES)), full((1, LANES)), full((1, SSM_INNER)), full((1, SSM_INNER)),
        ],
        out_specs=pl.BlockSpec((None, rows, SSM_INNER), lambda bi, c: (bi, c, 0)),
        out_shape=jax.ShapeDtypeStruct((b, s, SSM_INNER), jnp.bfloat16),
        scratch_shapes=[
            pltpu.VMEM((2 * SUBLANES, SSM_CONV_DIM), jnp.float32),
            pltpu.VMEM((L, SSM_INNER), jnp.float32),
            pltpu.VMEM((L, SSM_INNER), jnp.bfloat16),
            pltpu.VMEM((L, SSM_GROUPS * SSM_STATE), jnp.bfloat16),
            pltpu.VMEM((L, SSM_GROUPS * SSM_STATE), jnp.bfloat16),
            pltpu.VMEM((L, SSM_INNER), jnp.float32),
            pltpu.VMEM((SSM_GROUPS, SSM_STATE, SSM_HEADS_PER_GROUP * SSM_HEAD_DIM), jnp.float32),
        ],
        compiler_params=pltpu.CompilerParams(
            dimension_semantics=("arbitrary", "arbitrary"),
            vmem_limit_bytes=VMEM_LIMIT_BYTES),
        name="ssd",
    )(proj3, proj3, dt3, shift, conv_w, conv_b, dt_bias, a_log, d_skip_ch, norm_g)


def _out_kernel(x_ref, oatt_ref, yssm_ref, ga_ref, gs_ref, watt_ref, wssm_ref, wout_ref, gpost_ref, o_ref):
    y_att = jnp.dot(oatt_ref[...], watt_ref[...], preferred_element_type=jnp.float32)
    y_ssm = jnp.dot(yssm_ref[...], wssm_ref[...], preferred_element_type=jnp.float32)
    ga = ga_ref[...].astype(jnp.float32)
    gs = gs_ref[...].astype(jnp.float32)
    mixed = y_att / (1.0 + jnp.exp(-ga)) + y_ssm / (1.0 + jnp.exp(-gs))
    out = jnp.dot(mixed.astype(jnp.bfloat16), wout_ref[...], preferred_element_type=jnp.float32)
    ms = jnp.mean(out * out, axis=-1, keepdims=True)
    o_ref[...] = x_ref[...] + out * lax.rsqrt(ms + EPS) * gpost_ref[...]


def _out_proj(x2, o_att, y_ssm, proj2, w_att, w_ssm, w_out, g_post):
    m = x2.shape[0]
    tm = min(OUT_TM, m)
    const = lambda shape: pl.BlockSpec(shape, lambda i: (0, 0), pipeline_mode=pl.Buffered(1))
    return pl.pallas_call(
        _out_kernel,
        grid=(m // tm,),
        in_specs=[
            pl.BlockSpec((tm, D_MODEL), lambda i: (i, 0)),
            pl.BlockSpec((tm, ATT_WIDTH), lambda i: (i, 0)),
            pl.BlockSpec((tm, SSM_INNER), lambda i: (i, 0)),
            pl.BlockSpec((tm, D_MODEL), lambda i: (i, COL_GATE // D_MODEL)),
            pl.BlockSpec((tm, D_MODEL), lambda i: (i, COL_GATE // D_MODEL + 1)),
            const((ATT_WIDTH, D_MODEL)), const((SSM_INNER, D_MODEL)), const((D_MODEL, D_MODEL)),
            const((1, D_MODEL)),
        ],
        out_specs=pl.BlockSpec((tm, D_MODEL), lambda i: (i, 0)),
        out_shape=jax.ShapeDtypeStruct((m, D_MODEL), jnp.float32),
        compiler_params=pltpu.CompilerParams(
            dimension_semantics=("arbitrary",),
            vmem_limit_bytes=VMEM_LIMIT_BYTES),
        name="out_proj",
    )(x2, o_att, y_ssm, proj2, proj2, w_att, w_ssm, w_out, g_post)


def kernel(x, g_pre, w_in, att_lambda_q1, att_lambda_k1, att_lambda_q2, att_lambda_k2, att_subln_g, rel_bias,
           conv_w, conv_b, dt_bias, a_log, d_skip, ssm_norm_g, w_att_proj, w_ssm_proj, w_out, g_post):
    b, s, d = x.shape
    assert d == D_MODEL and g_pre.shape[0] == 1
    m = b * s
    f32, bf16 = jnp.float32, jnp.bfloat16
    t = min(ATT_BLOCK, s)
    assert s % t == 0 and s % SSM_CHUNK == 0

    w = w_in[0]
    o_q, o_k, o_v, o_g = 0, ATT_WIDTH, 2 * ATT_WIDTH, 3 * ATT_WIDTH
    o_z = 4 * ATT_WIDTH
    o_xbc = o_z + SSM_INNER
    o_dt = o_xbc + SSM_CONV_DIM
    o_gate = o_dt + SSM_HEADS
    w_main = jnp.concatenate([w[:, o_q:o_z], w[:, o_xbc:o_dt], w[:, o_z:o_xbc], w[:, o_gate:]], axis=1).astype(bf16)
    w_dt = jnp.pad(w[:, o_dt:o_gate], ((0, 0), (0, LANES - SSM_HEADS))).astype(bf16)
    col_scale = jnp.concatenate([jnp.full((1, ATT_WIDTH), ATT_QK_DIM ** -0.5 * LOG2E, f32),
                                 jnp.ones((1, PROJ_COLS - ATT_WIDTH), f32)], axis=1)

    x2 = x.reshape(m, d)
    proj, dt_raw = _in_proj(x2, g_pre, w_main, w_dt, col_scale)
    proj3 = proj.reshape(b, s, PROJ_COLS)
    dt3 = dt_raw.reshape(b, s, LANES)

    bias = _bias_tiles(rel_bias, t)
    o_att = _attention(proj3, bias, att_subln_g, att_lambda_q1, att_lambda_k1, att_lambda_q2, att_lambda_k2, t)

    pad_h = lambda v: jnp.pad(v, ((0, 0), (0, LANES - SSM_HEADS)))
    d_skip_ch = jnp.repeat(d_skip, SSM_HEAD_DIM, axis=1)
    y_ssm = _ssd(proj3, dt3, conv_w[0], conv_b, pad_h(dt_bias), pad_h(a_log), d_skip_ch, ssm_norm_g)

    out = _out_proj(x2, o_att.reshape(m, ATT_WIDTH), y_ssm.reshape(m, SSM_INNER), proj,
                    w_att_proj[0].astype(bf16), w_ssm_proj[0].astype(bf16), w_out[0].astype(bf16), g_post)
    return out.reshape(b, s, d)
```

```python
import functools
import math

import numpy as np
import jax
import jax.numpy as jnp
from jax import lax
from jax.experimental import pallas as pl
from jax.experimental.pallas import tpu as pltpu

D_MODEL = 1024
ATT_HEADS = 8
ATT_QK_DIM = 64
ATT_V_DIM = 2 * ATT_QK_DIM
ATT_WIDTH = ATT_HEADS * ATT_V_DIM
REL_BUCKETS = 32
REL_MAX_DIST = 128
SSM_INNER = 2 * D_MODEL
SSM_HEAD_DIM = 64
SSM_HEADS = SSM_INNER // SSM_HEAD_DIM
SSM_GROUPS = 8
SSM_HEADS_PER_GROUP = SSM_HEADS // SSM_GROUPS
SSM_STATE = 128
SSM_CONV = 4
SSM_CHUNK = 256
SSM_CONV_DIM = SSM_INNER + 2 * SSM_GROUPS * SSM_STATE
EPS = 1e-6
LAMBDA_INIT = 0.8 - 0.6 * math.exp(-0.3 * 0)

LANES = 128
SUBLANES = 8
VMEM_LIMIT_BYTES = 56 * 1024 * 1024
LOG2E = math.log2(math.e)
NEG_BIG = -1e30

COL_Q = 0
COL_K = COL_Q + ATT_WIDTH
COL_V = COL_K + ATT_WIDTH
COL_GATT = COL_V + ATT_WIDTH
COL_XBC = COL_GATT + ATT_WIDTH
COL_Z = COL_XBC + SSM_CONV_DIM
COL_GATE = COL_Z + SSM_INNER
PROJ_COLS = COL_GATE + 2 * D_MODEL

ATT_BLOCK = 512
N_BIAS_TILES = 3
ATT_SOFTMAX_ROWS = 64
ATT_ROW_GROUP = 16
ATT_UNROLL = 8
ATT_ACC_SLOTS = 4
PROJ_TM = 1024
PROJ_TN = 3072
OUT_TM = 1024
SSD_CHUNKS_PER_STEP = 2


def _t5_bucket_thresholds():
    max_exact = REL_BUCKETS // 2
    n = np.arange(0, 4 * REL_MAX_DIST)
    nf = np.maximum(n, 1).astype(np.float32)
    large = max_exact + (np.log(nf / np.float32(max_exact)) / np.float32(math.log(REL_MAX_DIST / max_exact))
                         * np.float32(REL_BUCKETS - max_exact)).astype(np.int32)
    large = np.minimum(large, REL_BUCKETS - 1)
    bucket = np.where(n < max_exact, n, large)
    assert np.all(np.diff(bucket) >= 0)
    thr = [int(np.argmax(bucket >= b)) for b in range(REL_BUCKETS)]
    assert bucket[thr[-1]] == REL_BUCKETS - 1
    return thr


_BUCKET_THR = _t5_bucket_thresholds()


def _in_proj_kernel(x_ref, g_ref, w_ref, wdt_ref, scale_ref, o_ref, dt_ref, h_scr):
    @pl.when(pl.program_id(1) == 0)
    def _():
        xf = x_ref[...]
        ms = jnp.mean(xf * xf, axis=-1, keepdims=True)
        h = (xf * lax.rsqrt(ms + EPS) * g_ref[...]).astype(jnp.bfloat16)
        h_scr[...] = h
        dt_ref[...] = jnp.dot(h, wdt_ref[...], preferred_element_type=jnp.float32)

    acc = jnp.dot(h_scr[...], w_ref[...], preferred_element_type=jnp.float32)
    o_ref[...] = (acc * scale_ref[...]).astype(o_ref.dtype)


def _in_proj(x2, g_pre, w_main, w_dt, col_scale):
    m = x2.shape[0]
    tm = min(PROJ_TM, m)
    tn = PROJ_TN
    return pl.pallas_call(
        _in_proj_kernel,
        grid=(m // tm, PROJ_COLS // tn),
        in_specs=[
            pl.BlockSpec((tm, D_MODEL), lambda i, j: (i, 0)),
            pl.BlockSpec((1, D_MODEL), lambda i, j: (0, 0)),
            pl.BlockSpec((D_MODEL, tn), lambda i, j: (0, j)),
            pl.BlockSpec((D_MODEL, LANES), lambda i, j: (0, 0)),
            pl.BlockSpec((1, tn), lambda i, j: (0, j)),
        ],
        out_specs=[
            pl.BlockSpec((tm, tn), lambda i, j: (i, j)),
            pl.BlockSpec((tm, LANES), lambda i, j: (i, 0)),
        ],
        out_shape=[
            jax.ShapeDtypeStruct((m, PROJ_COLS), jnp.bfloat16),
            jax.ShapeDtypeStruct((m, LANES), jnp.float32),
        ],
        scratch_shapes=[pltpu.VMEM((tm, D_MODEL), jnp.bfloat16)],
        compiler_params=pltpu.CompilerParams(
            dimension_semantics=("arbitrary", "arbitrary"),
            vmem_limit_bytes=VMEM_LIMIT_BYTES),
        name="in_proj",
    )(x2, g_pre, w_main, w_dt, col_scale)


def _bias_kernel(rb_ref, o_ref, *, t):
    h = pl.program_id(0)
    row = lax.broadcasted_iota(jnp.int32, (t, t), 0)
    col = lax.broadcasted_iota(jnp.int32, (t, t), 1)
    far = rb_ref[REL_BUCKETS - 1, h]
    for d in range(N_BIAS_TILES):
        dist = d * t + row - col
        val = jnp.full((t, t), rb_ref[0, h], jnp.float32)
        for b in range(1, REL_BUCKETS):
            val = jnp.where(dist >= _BUCKET_THR[b], rb_ref[b, h], val)
        val = (val - far) * LOG2E
        if d == 0:
            val = jnp.where(dist >= 0, val, NEG_BIG)
        o_ref[d] = val


def _bias_tiles(rel_bias, t):
    assert t + 1 >= _BUCKET_THR[-1]
    return pl.pallas_call(
        functools.partial(_bias_kernel, t=t),
        grid=(ATT_HEADS,),
        in_specs=[pl.BlockSpec(memory_space=pltpu.SMEM)],
        out_specs=pl.BlockSpec((None, N_BIAS_TILES, t, t), lambda h: (h, 0, 0, 0)),
        out_shape=jax.ShapeDtypeStruct((ATT_HEADS, N_BIAS_TILES, t, t), jnp.float32),
        compiler_params=pltpu.CompilerParams(dimension_semantics=("arbitrary",)),
        name="t5_bias",
    )(rel_bias)


def _attn_kernel(q_ref, k_ref, v_ref, gatt_ref, bias_ref, subln_ref, lq1_ref, lk1_ref, lq2_ref, lk2_ref,
                 o_ref, vaug_scr, m_scr, acc_scr, s0_scr, s1_scr, p0_scr, p1_scr, al0_scr, al1_scr, *, t, nq):
    dv = ATT_V_DIM
    s_bufs, p_bufs, al_bufs = (s0_scr, s1_scr), (p0_scr, p1_scr), (al0_scr, al1_scr)
    steps = [(qi, j) for qi in range(nq) for j in range(qi + 1)]
    n_steps = len(steps)

    vaug_scr[:, :dv] = v_ref[...]
    vaug_scr[:, dv:] = jnp.ones((v_ref.shape[0], dv), v_ref.dtype)
    acc_scr[...] = jnp.zeros(acc_scr.shape, jnp.float32)
    m_scr[...] = jnp.full(m_scr.shape, -jnp.inf, jnp.float32)

    def rows(idx):
        if isinstance(idx, int):
            return pl.ds(idx * t, t)
        return pl.ds(pl.multiple_of(idx * t, t), t)

    def stage_a(qi, j, s_dst):
        q = q_ref[rows(qi), :]
        lane = lax.broadcasted_iota(jnp.int32, q.shape, 1)
        zero = jnp.zeros_like(q)
        grp = (t // ATT_ROW_GROUP, ATT_ROW_GROUP)
        qs = jnp.stack([jnp.where(lane < ATT_QK_DIM, q, zero).reshape(*grp, q.shape[1]),
                        jnp.where(lane >= ATT_QK_DIM, q, zero).reshape(*grp, q.shape[1])],
                       axis=1).reshape(2 * t, q.shape[1])
        d = min(qi - j, N_BIAS_TILES - 1) if isinstance(qi, int) else jnp.minimum(qi - j, N_BIAS_TILES - 1)
        s = lax.dot_general(qs, k_ref[rows(j), :], (((1,), (1,)), ((), ())),
                            preferred_element_type=jnp.float32)
        s_dst[...] = (s.reshape(grp[0], 2, grp[1], t)
                      + bias_ref[d].reshape(grp[0], 1, grp[1], t)).reshape(2 * t, t)

    def stage_b(qi, j, s_src, p_dst, al_dst):
        for r0 in range(0, 2 * t, ATT_SOFTMAX_ROWS):
            rs = slice(r0, r0 + ATT_SOFTMAX_ROWS)
            m_old = m_scr[rs, :]
            if isinstance(j, int):
                m_prev = jnp.full(m_old.shape, -jnp.inf, jnp.float32) if j == 0 else m_old
            else:
                m_prev = jnp.where(j == 0, -jnp.inf, m_old)
            m_new = jnp.maximum(m_prev, jnp.max(s_src[rs, :], axis=-1, keepdims=True))
            al_dst[rs, :] = jnp.exp2(m_prev - m_new)
            m_scr[rs, :] = m_new
        for r0 in range(0, 2 * t, ATT_SOFTMAX_ROWS):
            rs = slice(r0, r0 + ATT_SOFTMAX_ROWS)
            p_dst[rs, :] = jnp.exp2(s_src[rs, :] - m_scr[rs, :][:, :1]).astype(jnp.bfloat16)

    def finalize(qi):
        acc = acc_scr[qi % ATT_ACC_SLOTS] if isinstance(qi, int) else acc_scr[qi & (ATT_ACC_SLOTS - 1)]
        o = acc[:, :dv] / acc[:, dv:]
        o = o.reshape(t // ATT_ROW_GROUP, 2, ATT_ROW_GROUP, dv)
        lam = (jnp.exp(jnp.sum(lq1_ref[...] * lk1_ref[...], axis=-1, keepdims=True))
               - jnp.exp(jnp.sum(lq2_ref[...] * lk2_ref[...], axis=-1, keepdims=True))
               + LAMBDA_INIT)
        o = o[:, 0].reshape(t, dv) - lam * o[:, 1].reshape(t, dv)
        ms = jnp.mean(o * o, axis=-1, keepdims=True)
        o = o * lax.rsqrt(ms + EPS) * subln_ref[...] * (1.0 - LAMBDA_INIT)
        g = gatt_ref[rows(qi), :].astype(jnp.float32)
        o = o * (g / (1.0 + jnp.exp(-g)))
        o_ref[rows(qi), :] = o.astype(o_ref.dtype)

    def stage_c(qi, j, p_src, al_src):
        slot = qi % ATT_ACC_SLOTS if isinstance(qi, int) else qi & (ATT_ACC_SLOTS - 1)
        pv = jnp.dot(p_src[...], vaug_scr[rows(j), :], preferred_element_type=jnp.float32)
        al = al_src[...]
        acc_scr[slot] = acc_scr[slot] * jnp.concatenate([al, al], axis=1) + pv

    def iteration(par, a, b, c):
        if a is not None:
            stage_a(a[0], a[1], s_bufs[par])
        if b is not None:
            stage_b(b[0], b[1], s_bufs[1 - par], p_bufs[1 - par], al_bufs[1 - par])
        if c is not None:
            stage_c(c[0], c[1], p_bufs[par], al_bufs[par])

    def static_iteration(it):
        pick = lambda i: steps[i] if 0 <= i < n_steps else None
        iteration(it % 2, pick(it), pick(it - 1), pick(it - 2))
        c = pick(it - 2)
        if c is not None and c[0] == c[1]:
            finalize(c[0])

    def advance(qi, j):
        wrap = j == qi
        return jnp.where(wrap, qi + 1, qi), jnp.where(wrap, 0, j + 1)

    unroll = ATT_UNROLL
    assert unroll % 2 == 0 and ATT_ACC_SLOTS == 4
    first_q = unroll // 2 - 1
    lo = 2 + first_q * (first_q + 1) // 2
    lo += lo % 2
    n_trips = max(n_steps - lo, 0) // unroll
    loop_end = lo + unroll * n_trips if n_trips > 0 else 0
    if n_trips > 0:
        for it in range(lo):
            static_iteration(it)

        def trip(_, carry):
            a, b, c = carry[0:2], carry[2:4], carry[4:6]
            done1, done2 = jnp.bool_(False), jnp.bool_(False)
            q1, q2 = jnp.int32(0), jnp.int32(0)
            for k in range(unroll):
                iteration(k % 2, a, b, c)
                last = c[0] == c[1]
                first, second = last & ~done1, last & done1
                q1, q2 = jnp.where(first, c[0], q1), jnp.where(second, c[0], q2)
                done1, done2 = done1 | last, done2 | second
                a, b, c = advance(*a), a, b
            pl.when(done1)(lambda: finalize(q1))
            pl.when(done2)(lambda: finalize(q2))
            return (*a, *b, *c)

        init = tuple(jnp.int32(v) for i in (lo, lo - 1, lo - 2) for v in steps[i])
        lax.fori_loop(0, n_trips, trip, init)
    for it in range(loop_end, n_steps + 2):
        static_iteration(it)


def _attention(proj3, bias, subln_g, lq1, lk1, lq2, lk2, t):
    b, s, _ = proj3.shape
    hd = ATT_V_DIM
    kq, kk, kv, kg = COL_Q // hd, COL_K // hd, COL_V // hd, COL_GATT // hd
    vec = lambda n: pl.BlockSpec((1, n), lambda bi, h: (0, 0))
    head_cols = lambda first: pl.BlockSpec((None, s, hd), lambda bi, h: (bi, 0, first + h))
    return pl.pallas_call(
        functools.partial(_attn_kernel, t=t, nq=s // t),
        grid=(b, ATT_HEADS),
        in_specs=[
            head_cols(kq), head_cols(kk), head_cols(kv), head_cols(kg),
            pl.BlockSpec((None, N_BIAS_TILES, t, t), lambda bi, h: (h, 0, 0, 0)),
            vec(hd), vec(ATT_QK_DIM), vec(ATT_QK_DIM), vec(ATT_QK_DIM), vec(ATT_QK_DIM),
        ],
        out_specs=head_cols(0),
        out_shape=jax.ShapeDtypeStruct((b, s, ATT_WIDTH), jnp.bfloat16),
        scratch_shapes=[
            pltpu.VMEM((s, 2 * hd), jnp.bfloat16),
            pltpu.VMEM((2 * t, LANES), jnp.float32),
            pltpu.VMEM((ATT_ACC_SLOTS, 2 * t, 2 * hd), jnp.float32),
            pltpu.VMEM((2 * t, t), jnp.float32), pltpu.VMEM((2 * t, t), jnp.float32),
            pltpu.VMEM((2 * t, t), jnp.bfloat16), pltpu.VMEM((2 * t, t), jnp.bfloat16),
            pltpu.VMEM((2 * t, LANES), jnp.float32), pltpu.VMEM((2 * t, LANES), jnp.float32),
        ],
        compiler_params=pltpu.CompilerParams(
            dimension_semantics=("arbitrary", "arbitrary"),
            vmem_limit_bytes=VMEM_LIMIT_BYTES),
        name="diff_attn",
    )(proj3, proj3, proj3, proj3, bias, subln_g, lq1, lk1, lq2, lk2)


def _silu(x):
    h = 0.5 * x
    return h + h * jnp.tanh(h)


def _ssd_kernel(xbc_ref, z_ref, dt_ref, shift_ref, convw_ref, convb_ref, dtb_ref, alog_ref, dskip_ref, normg_ref,
                o_ref, tail_scr, xs_scr, xsb_scr, b_scr, c_scr, y_scr, st_scr):
    @pl.when(pl.program_id(1) == 0)
    def _():
        tail_scr[...] = jnp.zeros(tail_scr.shape, jnp.float32)
        st_scr[...] = jnp.zeros(st_scr.shape, jnp.float32)

    def chunk(ci, carry):
        rows = pl.ds(pl.multiple_of(ci * SSM_CHUNK, SSM_CHUNK), SSM_CHUNK)
        _ssd_chunk(rows, xbc_ref, z_ref, dt_ref, shift_ref, convw_ref, convb_ref, dtb_ref, alog_ref, dskip_ref,
                   normg_ref, o_ref, tail_scr, xs_scr, xsb_scr, b_scr, c_scr, y_scr, st_scr)
        return carry

    lax.fori_loop(0, SSD_CHUNKS_PER_STEP, chunk, 0)


def _ssd_chunk(rows, xbc_ref, z_ref, dt_ref, shift_ref, convw_ref, convb_ref, dtb_ref, alog_ref, dskip_ref,
               normg_ref, o_ref, tail_scr, xs_scr, xsb_scr, b_scr, c_scr, y_scr, st_scr):
    L = SSM_CHUNK
    P, N, G, HG = SSM_HEAD_DIM, SSM_STATE, SSM_GROUPS, SSM_HEADS_PER_GROUP
    halo = SUBLANES
    f32, bf16 = jnp.float32, jnp.bfloat16

    ct = 512
    for t0 in range(0, SSM_CONV_DIM, ct):
        cs = slice(t0, t0 + ct)
        w = [convw_ref[k:k + 1, cs] for k in range(SSM_CONV)]
        u = xbc_ref[rows, cs].astype(f32)
        taps = jnp.concatenate([(u * w[SSM_CONV - 1 - k]).astype(bf16) for k in range(1, SSM_CONV)], axis=0)
        acc = (convb_ref[:, cs] + w[SSM_CONV - 1] * u
               + jnp.dot(shift_ref[...], taps, preferred_element_type=f32))
        corr = sum(w[SSM_CONV - 1 - k] * tail_scr[halo - k:2 * halo - k, cs] for k in range(1, SSM_CONV))
        act = _silu(jnp.concatenate([acc[:halo] + corr, acc[halo:]], axis=0))
        tail_scr[:halo, cs] = u[L - halo:]
        if t0 < SSM_INNER:
            xs_scr[:, cs] = act
            xsb_scr[:, cs] = act.astype(bf16)
        elif t0 < SSM_INNER + G * N:
            b_scr[:, t0 - SSM_INNER:t0 - SSM_INNER + ct] = act.astype(bf16)
        else:
            c_scr[:, t0 - SSM_INNER - G * N:t0 - SSM_INNER - G * N + ct] = act.astype(bf16)

    xdt = dt_ref[rows, :] + dtb_ref[...]
    dtv = jnp.maximum(xdt, 0.0) + jnp.log1p(jnp.exp(-jnp.abs(xdt)))
    adt = dtv * (-jnp.exp(alog_ref[...]))
    row = lax.broadcasted_iota(jnp.int32, (L, L), 0)
    col = lax.broadcasted_iota(jnp.int32, (L, L), 1)
    tril = row >= col
    tri = jnp.where(tril, 1.0, 0.0).astype(bf16)
    hi = adt.astype(bf16)
    r1 = adt - hi.astype(f32)
    mid = r1.astype(bf16)
    lo = (r1 - mid.astype(f32)).astype(bf16)
    a_cs = (jnp.dot(tri, hi, preferred_element_type=f32)
            + jnp.dot(tri, mid, preferred_element_type=f32)
            + jnp.dot(tri, lo, preferred_element_type=f32))
    a_last = a_cs[L - 1:L, :]
    exp_last = jnp.exp(a_last)
    a2 = a_cs * LOG2E
    src2_t = (a2 - jnp.log2(dtv)).T
    w_end_t = (jnp.exp(a_last - a_cs) * dtv).T

    lane = lax.broadcasted_iota(jnp.int32, (L, LANES), 1)
    lane_row = lax.broadcasted_iota(jnp.int32, (1, LANES), 1)
    n_lt = L // LANES
    tril_blk = (lax.broadcasted_iota(jnp.int32, (LANES, LANES), 0)
                >= lax.broadcasted_iota(jnp.int32, (LANES, LANES), 1))

    def decay_block(cb, a_bc2, src_row, bi, bj):
        if bj > bi:
            return jnp.zeros((LANES, LANES), bf16)
        rs, cs = slice(bi * LANES, (bi + 1) * LANES), slice(bj * LANES, (bj + 1) * LANES)
        seg = a_bc2[rs, :] - src_row[:, cs]
        if bj == bi:
            seg = jnp.where(tril_blk, seg, -jnp.inf)
        return (cb[rs, cs] * jnp.exp2(seg)).astype(bf16)

    for g in range(G):
        b_g = b_scr[:, g * N:(g + 1) * N]
        c_g = c_scr[:, g * N:(g + 1) * N]
        cb = lax.dot_general(c_g, b_g, (((1,), (1,)), ((), ())), preferred_element_type=f32)
        b_g_t = b_g.astype(f32).T
        st_g = st_scr[g]
        y_off = jnp.dot(c_g, st_g.astype(bf16), preferred_element_type=f32)
        for pr in range(HG // 2):
            h0 = g * HG + 2 * pr
            cols = slice(h0 * P, (h0 + 2) * P)
            lcols = slice(2 * pr * P, (2 * pr + 2) * P)
            m_parts, bt_parts, ea_parts = [], [], []
            for h in (h0, h0 + 1):
                a_bc2 = jnp.broadcast_to(a2[:, h:h + 1], (L, LANES))
                src_row = src2_t[h:h + 1, :]
                m_parts.append(jnp.concatenate(
                    [jnp.concatenate([decay_block(cb, a_bc2, src_row, bi, bj) for bj in range(n_lt)], axis=1)
                     for bi in range(n_lt)], axis=0))
                bt_parts.append((b_g_t * w_end_t[h:h + 1, :]).astype(bf16))
                ea_parts.append(jnp.exp2(a_bc2))
            xsb = xsb_scr[:, cols]
            zero = jnp.zeros_like(xsb)
            lane_b = lax.broadcasted_iota(jnp.int32, xsb.shape, 1)
            x_bd = jnp.concatenate([jnp.where(lane_b < P, xsb, zero),
                                    jnp.where(lane_b >= P, xsb, zero)], axis=0)
            y_pair = jnp.dot(jnp.concatenate(m_parts, axis=1), x_bd, preferred_element_type=f32)
            ea = jnp.where(lane < P, ea_parts[0], ea_parts[1])
            y_scr[:, cols] = y_pair + y_off[:, lcols] * ea + xs_scr[:, cols] * dskip_ref[:, cols]
            upd = jnp.dot(jnp.concatenate(bt_parts, axis=1), x_bd, preferred_element_type=f32)
            el = jnp.where(lane_row < P, exp_last[:, h0:h0 + 1], exp_last[:, h0 + 1:h0 + 2])
            st_scr[g, :, lcols] = st_g[:, lcols] * el + upd

    gw = SSM_INNER // G
    for g in range(G):
        gs = slice(g * gw, (g + 1) * gw)
        zz = z_ref[rows, gs].astype(f32)
        y = y_scr[:, gs] * _silu(zz)
        ms = jnp.mean(y * y, axis=-1, keepdims=True)
        o_ref[rows, gs] = (y * lax.rsqrt(ms + EPS) * normg_ref[:, gs]).astype(o_ref.dtype)


def _ssd(proj3, dt3, conv_w, conv_b, dt_bias, a_log, d_skip_ch, norm_g):
    b, s, _ = proj3.shape
    L = SSM_CHUNK
    assert s % L == 0
    full = lambda shape: pl.BlockSpec(shape, lambda bi, c: (0,) * len(shape))
    t_idx = np.arange(L)
    shift = np.concatenate([(t_idx[None, :] == t_idx[:, None] - k) for k in range(1, SSM_CONV)], axis=1)
    shift = jnp.asarray(shift, jnp.bfloat16)
    rows = L * SSD_CHUNKS_PER_STEP
    assert s % rows == 0
    return pl.pallas_call(
        _ssd_kernel,
        grid=(b, s // rows),
        in_specs=[
            pl.BlockSpec((None, rows, SSM_CONV_DIM), lambda bi, c: (bi, c, COL_XBC // SSM_CONV_DIM)),
            pl.BlockSpec((None, rows, SSM_INNER), lambda bi, c: (bi, c, COL_Z // SSM_INNER)),
            pl.BlockSpec((None, rows, LANES), lambda bi, c: (bi, c, 0)),
            full((L, (SSM_CONV - 1) * L)),
            full((SSM_CONV, SSM_CONV_DIM)), full((1, SSM_CONV_DIM)),
            full((1, LANES)), full((1, LANES)), full((1, SSM_INNER)), full((1, SSM_INNER)),
        ],
        out_specs=pl.BlockSpec((None, rows, SSM_INNER), lambda bi, c: (bi, c, 0)),
        out_shape=jax.ShapeDtypeStruct((b, s, SSM_INNER), jnp.bfloat16),
        scratch_shapes=[
            pltpu.VMEM((2 * SUBLANES, SSM_CONV_DIM), jnp.float32),
            pltpu.VMEM((L, SSM_INNER), jnp.float32),
            pltpu.VMEM((L, SSM_INNER), jnp.bfloat16),
            pltpu.VMEM((L, SSM_GROUPS * SSM_STATE), jnp.bfloat16),
            pltpu.VMEM((L, SSM_GROUPS * SSM_STATE), jnp.bfloat16),
            pltpu.VMEM((L, SSM_INNER), jnp.float32),
            pltpu.VMEM((SSM_GROUPS, SSM_STATE, SSM_HEADS_PER_GROUP * SSM_HEAD_DIM), jnp.float32),
        ],
        compiler_params=pltpu.CompilerParams(
            dimension_semantics=("arbitrary", "arbitrary"),
            vmem_limit_bytes=VMEM_LIMIT_BYTES),
        name="ssd",
    )(proj3, proj3, dt3, shift, conv_w, conv_b, dt_bias, a_log, d_skip_ch, norm_g)


def _out_kernel(x_ref, oatt_ref, yssm_ref, ga_ref, gs_ref, watt_ref, wssm_ref, wout_ref, gpost_ref, o_ref):
    y_att = jnp.dot(oatt_ref[...], watt_ref[...], preferred_element_type=jnp.float32)
    y_ssm = jnp.dot(yssm_ref[...], wssm_ref[...], preferred_element_type=jnp.float32)
    ga = ga_ref[...].astype(jnp.float32)
    gs = gs_ref[...].astype(jnp.float32)
    mixed = y_att / (1.0 + jnp.exp(-ga)) + y_ssm / (1.0 + jnp.exp(-gs))
    out = jnp.dot(mixed.astype(jnp.bfloat16), wout_ref[...], preferred_element_type=jnp.float32)
    ms = jnp.mean(out * out, axis=-1, keepdims=True)
    o_ref[...] = x_ref[...] + out * lax.rsqrt(ms + EPS) * gpost_ref[...]


def _out_proj(x2, o_att, y_ssm, proj2, w_att, w_ssm, w_out, g_post):
    m = x2.shape[0]
    tm = min(OUT_TM, m)
    const = lambda shape: pl.BlockSpec(shape, lambda i: (0, 0), pipeline_mode=pl.Buffered(1))
    return pl.pallas_call(
        _out_kernel,
        grid=(m // tm,),
        in_specs=[
            pl.BlockSpec((tm, D_MODEL), lambda i: (i, 0)),
            pl.BlockSpec((tm, ATT_WIDTH), lambda i: (i, 0)),
            pl.BlockSpec((tm, SSM_INNER), lambda i: (i, 0)),
            pl.BlockSpec((tm, D_MODEL), lambda i: (i, COL_GATE // D_MODEL)),
            pl.BlockSpec((tm, D_MODEL), lambda i: (i, COL_GATE // D_MODEL + 1)),
            const((ATT_WIDTH, D_MODEL)), const((SSM_INNER, D_MODEL)), const((D_MODEL, D_MODEL)),
            const((1, D_MODEL)),
        ],
        out_specs=pl.BlockSpec((tm, D_MODEL), lambda i: (i, 0)),
        out_shape=jax.ShapeDtypeStruct((m, D_MODEL), jnp.float32),
        compiler_params=pltpu.CompilerParams(
            dimension_semantics=("arbitrary",),
            vmem_limit_bytes=VMEM_LIMIT_BYTES),
        name="out_proj",
    )(x2, o_att, y_ssm, proj2, proj2, w_att, w_ssm, w_out, g_post)


def kernel(x, g_pre, w_in, att_lambda_q1, att_lambda_k1, att_lambda_q2, att_lambda_k2, att_subln_g, rel_bias,
           conv_w, conv_b, dt_bias, a_log, d_skip, ssm_norm_g, w_att_proj, w_ssm_proj, w_out, g_post):
    b, s, d = x.shape
    assert d == D_MODEL and g_pre.shape[0] == 1
    m = b * s
    f32, bf16 = jnp.float32, jnp.bfloat16
    t = min(ATT_BLOCK, s)
    assert s % t == 0 and s % SSM_CHUNK == 0

    w = w_in[0]
    o_q, o_k, o_v, o_g = 0, ATT_WIDTH, 2 * ATT_WIDTH, 3 * ATT_WIDTH
    o_z = 4 * ATT_WIDTH
    o_xbc = o_z + SSM_INNER
    o_dt = o_xbc + SSM_CONV_DIM
    o_gate = o_dt + SSM_HEADS
    w_main = jnp.concatenate([w[:, o_q:o_z], w[:, o_xbc:o_dt], w[:, o_z:o_xbc], w[:, o_gate:]], axis=1).astype(bf16)
    w_dt = jnp.pad(w[:, o_dt:o_gate], ((0, 0), (0, LANES - SSM_HEADS))).astype(bf16)
    col_scale = jnp.concatenate([jnp.full((1, ATT_WIDTH), ATT_QK_DIM ** -0.5 * LOG2E, f32),
                                 jnp.ones((1, PROJ_COLS - ATT_WIDTH), f32)], axis=1)

    x2 = x.reshape(m, d)
    proj, dt_raw = _in_proj(x2, g_pre, w_main, w_dt, col_scale)
    proj3 = proj.reshape(b, s, PROJ_COLS)
    dt3 = dt_raw.reshape(b, s, LANES)

    bias = _bias_tiles(rel_bias, t)
    o_att = _attention(proj3, bias, att_subln_g, att_lambda_q1, att_lambda_k1, att_lambda_q2, att_lambda_k2, t)

    pad_h = lambda v: jnp.pad(v, ((0, 0), (0, LANES - SSM_HEADS)))
    d_skip_ch = jnp.repeat(d_skip, SSM_HEAD_DIM, axis=1)
    y_ssm = _ssd(proj3, dt3, conv_w[0], conv_b, pad_h(dt_bias), pad_h(a_log), d_skip_ch, ssm_norm_g)

    out = _out_proj(x2, o_att.reshape(m, ATT_WIDTH), y_ssm.reshape(m, SSM_INNER), proj,
                    w_att_proj[0].astype(bf16), w_ssm_proj[0].astype(bf16), w_out[0].astype(bf16), g_post)
    return out.reshape(b, s, d)
```

```python
import functools
import math

import numpy as np
import jax
import jax.numpy as jnp
from jax import lax
from jax.experimental import pallas as pl
from jax.experimental.pallas import tpu as pltpu

D_MODEL = 1024
ATT_HEADS = 8
ATT_QK_DIM = 64
ATT_V_DIM = 2 * ATT_QK_DIM
ATT_WIDTH = ATT_HEADS * ATT_V_DIM
REL_BUCKETS = 32
REL_MAX_DIST = 128
SSM_INNER = 2 * D_MODEL
SSM_HEAD_DIM = 64
SSM_HEADS = SSM_INNER // SSM_HEAD_DIM
SSM_GROUPS = 8
SSM_HEADS_PER_GROUP = SSM_HEADS // SSM_GROUPS
SSM_STATE = 128
SSM_CONV = 4
SSM_CHUNK = 256
SSM_CONV_DIM = SSM_INNER + 2 * SSM_GROUPS * SSM_STATE
EPS = 1e-6
LAMBDA_INIT = 0.8 - 0.6 * math.exp(-0.3 * 0)

LANES = 128
SUBLANES = 8
VMEM_LIMIT_BYTES = 56 * 1024 * 1024
LOG2E = math.log2(math.e)
NEG_BIG = -1e30

COL_Q = 0
COL_K = COL_Q + ATT_WIDTH
COL_V = COL_K + ATT_WIDTH
COL_GATT = COL_V + ATT_WIDTH
COL_XBC = COL_GATT + ATT_WIDTH
COL_Z = COL_XBC + SSM_CONV_DIM
COL_GATE = COL_Z + SSM_INNER
PROJ_COLS = COL_GATE + 2 * D_MODEL

ATT_BLOCK = 512
N_BIAS_TILES = 3
ATT_SOFTMAX_ROWS = 64
ATT_ROW_GROUP = 16
ATT_UNROLL = 8
ATT_ACC_SLOTS = 4
PROJ_TM = 1024
PROJ_TN = 3072
OUT_TM = 1024
SSD_CHUNKS_PER_STEP = 2


def _t5_bucket_thresholds():
    max_exact = REL_BUCKETS // 2
    n = np.arange(0, 4 * REL_MAX_DIST)
    nf = np.maximum(n, 1).astype(np.float32)
    large = max_exact + (np.log(nf / np.float32(max_exact)) / np.float32(math.log(REL_MAX_DIST / max_exact))
                         * np.float32(REL_BUCKETS - max_exact)).astype(np.int32)
    large = np.minimum(large, REL_BUCKETS - 1)
    bucket = np.where(n < max_exact, n, large)
    assert np.all(np.diff(bucket) >= 0)
    thr = [int(np.argmax(bucket >= b)) for b in range(REL_BUCKETS)]
    assert bucket[thr[-1]] == REL_BUCKETS - 1
    return thr


_BUCKET_THR = _t5_bucket_thresholds()


def _in_proj_kernel(x_ref, g_ref, w_ref, wdt_ref, scale_ref, o_ref, dt_ref, h_scr):
    @pl.when(pl.program_id(1) == 0)
    def _():
        xf = x_ref[...]
        ms = jnp.mean(xf * xf, axis=-1, keepdims=True)
        h = (xf * lax.rsqrt(ms + EPS) * g_ref[...]).astype(jnp.bfloat16)
        h_scr[...] = h
        dt_ref[...] = jnp.dot(h, wdt_ref[...], preferred_element_type=jnp.float32)

    acc = jnp.dot(h_scr[...], w_ref[...], preferred_element_type=jnp.float32)
    o_ref[...] = (acc * scale_ref[...]).astype(o_ref.dtype)


def _in_proj(x2, g_pre, w_main, w_dt, col_scale):
    m = x2.shape[0]
    tm = min(PROJ_TM, m)
    tn = PROJ_TN
    return pl.pallas_call(
        _in_proj_kernel,
        grid=(m // tm, PROJ_COLS // tn),
        in_specs=[
            pl.BlockSpec((tm, D_MODEL), lambda i, j: (i, 0)),
            pl.BlockSpec((1, D_MODEL), lambda i, j: (0, 0)),
            pl.BlockSpec((D_MODEL, tn), lambda i, j: (0, j)),
            pl.BlockSpec((D_MODEL, LANES), lambda i, j: (0, 0)),
            pl.BlockSpec((1, tn), lambda i, j: (0, j)),
        ],
        out_specs=[
            pl.BlockSpec((tm, tn), lambda i, j: (i, j)),
            pl.BlockSpec((tm, LANES), lambda i, j: (i, 0)),
        ],
        out_shape=[
            jax.ShapeDtypeStruct((m, PROJ_COLS), jnp.bfloat16),
            jax.ShapeDtypeStruct((m, LANES), jnp.float32),
        ],
        scratch_shapes=[pltpu.VMEM((tm, D_MODEL), jnp.bfloat16)],
        compiler_params=pltpu.CompilerParams(
            dimension_semantics=("arbitrary", "arbitrary"),
            vmem_limit_bytes=VMEM_LIMIT_BYTES),
        name="in_proj",
    )(x2, g_pre, w_main, w_dt, col_scale)


def _bias_kernel(rb_ref, o_ref, *, t):
    h = pl.program_id(0)
    row = lax.broadcasted_iota(jnp.int32, (t, t), 0)
    col = lax.broadcasted_iota(jnp.int32, (t, t), 1)
    far = rb_ref[REL_BUCKETS - 1, h]
    for d in range(N_BIAS_TILES):
        dist = d * t + row - col
        val = jnp.full((t, t), rb_ref[0, h], jnp.float32)
        for b in range(1, REL_BUCKETS):
            val = jnp.where(dist >= _BUCKET_THR[b], rb_ref[b, h], val)
        val = (val - far) * LOG2E
        if d == 0:
            val = jnp.where(dist >= 0, val, NEG_BIG)
        o_ref[d] = val


def _bias_tiles(rel_bias, t):
    assert t + 1 >= _BUCKET_THR[-1]
    return pl.pallas_call(
        functools.partial(_bias_kernel, t=t),
        grid=(ATT_HEADS,),
        in_specs=[pl.BlockSpec(memory_space=pltpu.SMEM)],
        out_specs=pl.BlockSpec((None, N_BIAS_TILES, t, t), lambda h: (h, 0, 0, 0)),
        out_shape=jax.ShapeDtypeStruct((ATT_HEADS, N_BIAS_TILES, t, t), jnp.float32),
        compiler_params=pltpu.CompilerParams(dimension_semantics=("arbitrary",)),
        name="t5_bias",
    )(rel_bias)


def _attn_kernel(q_ref, k_ref, v_ref, gatt_ref, bias_ref, subln_ref, lq1_ref, lk1_ref, lq2_ref, lk2_ref,
                 o_ref, vaug_scr, m_scr, acc_scr, s0_scr, s1_scr, p0_scr, p1_scr, al0_scr, al1_scr, *, t, nq):
    dv = ATT_V_DIM
    s_bufs, p_bufs, al_bufs = (s0_scr, s1_scr), (p0_scr, p1_scr), (al0_scr, al1_scr)
    steps = [(qi, j) for qi in range(nq) for j in range(qi + 1)]
    n_steps = len(steps)

    vaug_scr[:, :dv] = v_ref[...]
    vaug_scr[:, dv:] = jnp.ones((v_ref.shape[0], dv), v_ref.dtype)
    acc_scr[...] = jnp.zeros(acc_scr.shape, jnp.float32)
    m_scr[...] = jnp.full(m_scr.shape, -jnp.inf, jnp.float32)

    def rows(idx):
        if isinstance(idx, int):
            return pl.ds(idx * t, t)
        return pl.ds(pl.multiple_of(idx * t, t), t)

    def stage_a(qi, j, s_dst):
        q = q_ref[rows(qi), :]
        lane = lax.broadcasted_iota(jnp.int32, q.shape, 1)
        zero = jnp.zeros_like(q)
        grp = (t // ATT_ROW_GROUP, ATT_ROW_GROUP)
        qs = jnp.stack([jnp.where(lane < ATT_QK_DIM, q, zero).reshape(*grp, q.shape[1]),
                        jnp.where(lane >= ATT_QK_DIM, q, zero).reshape(*grp, q.shape[1])],
                       axis=1).reshape(2 * t, q.shape[1])
        d = min(qi - j, N_BIAS_TILES - 1) if isinstance(qi, int) else jnp.minimum(qi - j, N_BIAS_TILES - 1)
        s = lax.dot_general(qs, k_ref[rows(j), :], (((1,), (1,)), ((), ())),
                            preferred_element_type=jnp.float32)
        s_dst[...] = (s.reshape(grp[0], 2, grp[1], t)
                      + bias_ref[d].reshape(grp[0], 1, grp[1], t)).reshape(2 * t, t)

    def stage_b(qi, j, s_src, p_dst, al_dst):
        for r0 in range(0, 2 * t, ATT_SOFTMAX_ROWS):
            rs = slice(r0, r0 + ATT_SOFTMAX_ROWS)
            m_old = m_scr[rs, :]
            if isinstance(j, int):
                m_prev = jnp.full(m_old.shape, -jnp.inf, jnp.float32) if j == 0 else m_old
            else:
                m_prev = jnp.where(j == 0, -jnp.inf, m_old)
            m_new = jnp.maximum(m_prev, jnp.max(s_src[rs, :], axis=-1, keepdims=True))
            al_dst[rs, :] = jnp.exp2(m_prev - m_new)
            m_scr[rs, :] = m_new
        for r0 in range(0, 2 * t, ATT_SOFTMAX_ROWS):
            rs = slice(r0, r0 + ATT_SOFTMAX_ROWS)
            p_dst[rs, :] = jnp.exp2(s_src[rs, :] - m_scr[rs, :][:, :1]).astype(jnp.bfloat16)

    def finalize(qi):
        acc = acc_scr[qi % ATT_ACC_SLOTS] if isinstance(qi, int) else acc_scr[qi & (ATT_ACC_SLOTS - 1)]
        o = acc[:, :dv] / acc[:, dv:]
        o = o.reshape(t // ATT_ROW_GROUP, 2, ATT_ROW_GROUP, dv)
        lam = (jnp.exp(jnp.sum(lq1_ref[...] * lk1_ref[...], axis=-1, keepdims=True))
               - jnp.exp(jnp.sum(lq2_ref[...] * lk2_ref[...], axis=-1, keepdims=True))
               + LAMBDA_INIT)
        o = o[:, 0].reshape(t, dv) - lam * o[:, 1].reshape(t, dv)
        ms = jnp.mean(o * o, axis=-1, keepdims=True)
        o = o * lax.rsqrt(ms + EPS) * subln_ref[...] * (1.0 - LAMBDA_INIT)
        o = o * _silu(gatt_ref[rows(qi), :].astype(jnp.float32))
        o_ref[rows(qi), :] = o.astype(o_ref.dtype)

    def stage_c(qi, j, p_src, al_src):
        slot = qi % ATT_ACC_SLOTS if isinstance(qi, int) else qi & (ATT_ACC_SLOTS - 1)
        pv = jnp.dot(p_src[...], vaug_scr[rows(j), :], preferred_element_type=jnp.float32)
        al = al_src[...]
        acc_scr[slot] = acc_scr[slot] * jnp.concatenate([al, al], axis=1) + pv

    def iteration(par, a, b, c):
        if a is not None:
            stage_a(a[0], a[1], s_bufs[par])
        if b is not None:
            stage_b(b[0], b[1], s_bufs[1 - par], p_bufs[1 - par], al_bufs[1 - par])
        if c is not None:
            stage_c(c[0], c[1], p_bufs[par], al_bufs[par])

    def static_iteration(it):
        pick = lambda i: steps[i] if 0 <= i < n_steps else None
        iteration(it % 2, pick(it), pick(it - 1), pick(it - 2))
        c = pick(it - 2)
        if c is not None and c[0] == c[1]:
            finalize(c[0])

    def advance(qi, j):
        wrap = j == qi
        return jnp.where(wrap, qi + 1, qi), jnp.where(wrap, 0, j + 1)

    unroll = ATT_UNROLL
    assert unroll % 2 == 0 and ATT_ACC_SLOTS == 4
    first_q = unroll // 2 - 1
    lo = 2 + first_q * (first_q + 1) // 2
    lo += lo % 2
    n_trips = max(n_steps - lo, 0) // unroll
    loop_end = lo + unroll * n_trips if n_trips > 0 else 0
    if n_trips > 0:
        for it in range(lo):
            static_iteration(it)

        def trip(_, carry):
            a, b, c = carry[0:2], carry[2:4], carry[4:6]
            done1, done2 = jnp.bool_(False), jnp.bool_(False)
            q1, q2 = jnp.int32(0), jnp.int32(0)
            for k in range(unroll):
                iteration(k % 2, a, b, c)
                last = c[0] == c[1]
                first, second = last & ~done1, last & done1
                q1, q2 = jnp.where(first, c[0], q1), jnp.where(second, c[0], q2)
                done1, done2 = done1 | last, done2 | second
                a, b, c = advance(*a), a, b
            pl.when(done1)(lambda: finalize(q1))
            pl.when(done2)(lambda: finalize(q2))
            return (*a, *b, *c)

        init = tuple(jnp.int32(v) for i in (lo, lo - 1, lo - 2) for v in steps[i])
        lax.fori_loop(0, n_trips, trip, init)
    for it in range(loop_end, n_steps + 2):
        static_iteration(it)


def _attention(proj3, bias, subln_g, lq1, lk1, lq2, lk2, t):
    b, s, _ = proj3.shape
    hd = ATT_V_DIM
    kq, kk, kv, kg = COL_Q // hd, COL_K // hd, COL_V // hd, COL_GATT // hd
    vec = lambda n: pl.BlockSpec((1, n), lambda bi, h: (0, 0))
    head_cols = lambda first: pl.BlockSpec((None, s, hd), lambda bi, h: (bi, 0, first + h))
    return pl.pallas_call(
        functools.partial(_attn_kernel, t=t, nq=s // t),
        grid=(b, ATT_HEADS),
        in_specs=[
            head_cols(kq), head_cols(kk), head_cols(kv), head_cols(kg),
            pl.BlockSpec((None, N_BIAS_TILES, t, t), lambda bi, h: (h, 0, 0, 0)),
            vec(hd), vec(ATT_QK_DIM), vec(ATT_QK_DIM), vec(ATT_QK_DIM), vec(ATT_QK_DIM),
        ],
        out_specs=head_cols(0),
        out_shape=jax.ShapeDtypeStruct((b, s, ATT_WIDTH), jnp.bfloat16),
        scratch_shapes=[
            pltpu.VMEM((s, 2 * hd), jnp.bfloat16),
            pltpu.VMEM((2 * t, LANES), jnp.float32),
            pltpu.VMEM((ATT_ACC_SLOTS, 2 * t, 2 * hd), jnp.float32),
            pltpu.VMEM((2 * t, t), jnp.float32), pltpu.VMEM((2 * t, t), jnp.float32),
            pltpu.VMEM((2 * t, t), jnp.bfloat16), pltpu.VMEM((2 * t, t), jnp.bfloat16),
            pltpu.VMEM((2 * t, LANES), jnp.float32), pltpu.VMEM((2 * t, LANES), jnp.float32),
        ],
        compiler_params=pltpu.CompilerParams(
            dimension_semantics=("arbitrary", "arbitrary"),
            vmem_limit_bytes=VMEM_LIMIT_BYTES),
        name="diff_attn",
    )(proj3, proj3, proj3, proj3, bias, subln_g, lq1, lk1, lq2, lk2)


def _silu(x):
    h = 0.5 * x
    return h + h * jnp.tanh(h)


def _ssd_kernel(xbc_ref, z_ref, dt_ref, shift_ref, convw_ref, convb_ref, dtb_ref, alog_ref, dskip_ref, normg_ref,
                o_ref, tail_scr, xs_scr, xsb_scr, b_scr, c_scr, y_scr, st_scr):
    @pl.when(pl.program_id(1) == 0)
    def _():
        tail_scr[...] = jnp.zeros(tail_scr.shape, jnp.float32)
        st_scr[...] = jnp.zeros(st_scr.shape, jnp.float32)

    def chunk(ci, carry):
        rows = pl.ds(pl.multiple_of(ci * SSM_CHUNK, SSM_CHUNK), SSM_CHUNK)
        _ssd_chunk(rows, xbc_ref, z_ref, dt_ref, shift_ref, convw_ref, convb_ref, dtb_ref, alog_ref, dskip_ref,
                   normg_ref, o_ref, tail_scr, xs_scr, xsb_scr, b_scr, c_scr, y_scr, st_scr)
        return carry

    lax.fori_loop(0, SSD_CHUNKS_PER_STEP, chunk, 0)


def _ssd_chunk(rows, xbc_ref, z_ref, dt_ref, shift_ref, convw_ref, convb_ref, dtb_ref, alog_ref, dskip_ref,
               normg_ref, o_ref, tail_scr, xs_scr, xsb_scr, b_scr, c_scr, y_scr, st_scr):
    L = SSM_CHUNK
    P, N, G, HG = SSM_HEAD_DIM, SSM_STATE, SSM_GROUPS, SSM_HEADS_PER_GROUP
    halo = SUBLANES
    f32, bf16 = jnp.float32, jnp.bfloat16

    ct = 512
    for t0 in range(0, SSM_CONV_DIM, ct):
        cs = slice(t0, t0 + ct)
        w = [convw_ref[k:k + 1, cs] for k in range(SSM_CONV)]
        u = xbc_ref[rows, cs].astype(f32)
        taps = jnp.concatenate([(u * w[SSM_CONV - 1 - k]).astype(bf16) for k in range(1, SSM_CONV)], axis=0)
        acc = (convb_ref[:, cs] + w[SSM_CONV - 1] * u
               + jnp.dot(shift_ref[...], taps, preferred_element_type=f32))
        corr = sum(w[SSM_CONV - 1 - k] * tail_scr[halo - k:2 * halo - k, cs] for k in range(1, SSM_CONV))
        act = _silu(jnp.concatenate([acc[:halo] + corr, acc[halo:]], axis=0))
        tail_scr[:halo, cs] = u[L - halo:]
        if t0 < SSM_INNER:
            xs_scr[:, cs] = act
            xsb_scr[:, cs] = act.astype(bf16)
        elif t0 < SSM_INNER + G * N:
            b_scr[:, t0 - SSM_INNER:t0 - SSM_INNER + ct] = act.astype(bf16)
        else:
            c_scr[:, t0 - SSM_INNER - G * N:t0 - SSM_INNER - G * N + ct] = act.astype(bf16)

    xdt = dt_ref[rows, :] + dtb_ref[...]
    dtv = jnp.maximum(xdt, 0.0) + jnp.log1p(jnp.exp(-jnp.abs(xdt)))
    adt = dtv * (-jnp.exp(alog_ref[...]))
    row = lax.broadcasted_iota(jnp.int32, (L, L), 0)
    col = lax.broadcasted_iota(jnp.int32, (L, L), 1)
    tril = row >= col
    tri = jnp.where(tril, 1.0, 0.0).astype(bf16)
    hi = adt.astype(bf16)
    r1 = adt - hi.astype(f32)
    mid = r1.astype(bf16)
    lo = (r1 - mid.astype(f32)).astype(bf16)
    a_cs = (jnp.dot(tri, hi, preferred_element_type=f32)
            + jnp.dot(tri, mid, preferred_element_type=f32)
            + jnp.dot(tri, lo, preferred_element_type=f32))
    a_last = a_cs[L - 1:L, :]
    exp_last = jnp.exp(a_last)
    a2 = a_cs * LOG2E
    src2_t = (a2 - jnp.log2(dtv)).T
    w_end_t = (jnp.exp(a_last - a_cs) * dtv).T

    lane = lax.broadcasted_iota(jnp.int32, (L, LANES), 1)
    lane_row = lax.broadcasted_iota(jnp.int32, (1, LANES), 1)
    n_lt = L // LANES
    tril_blk = (lax.broadcasted_iota(jnp.int32, (LANES, LANES), 0)
                >= lax.broadcasted_iota(jnp.int32, (LANES, LANES), 1))

    def decay_block(cb, a_bc2, src_row, bi, bj):
        if bj > bi:
            return jnp.zeros((LANES, LANES), bf16)
        rs, cs = slice(bi * LANES, (bi + 1) * LANES), slice(bj * LANES, (bj + 1) * LANES)
        seg = a_bc2[rs, :] - src_row[:, cs]
        if bj == bi:
            seg = jnp.where(tril_blk, seg, -jnp.inf)
        return (cb[rs, cs] * jnp.exp2(seg)).astype(bf16)

    for g in range(G):
        b_g = b_scr[:, g * N:(g + 1) * N]
        c_g = c_scr[:, g * N:(g + 1) * N]
        cb = lax.dot_general(c_g, b_g, (((1,), (1,)), ((), ())), preferred_element_type=f32)
        b_g_t = b_g.astype(f32).T
        st_g = st_scr[g]
        y_off = jnp.dot(c_g, st_g.astype(bf16), preferred_element_type=f32)
        for pr in range(HG // 2):
            h0 = g * HG + 2 * pr
            cols = slice(h0 * P, (h0 + 2) * P)
            lcols = slice(2 * pr * P, (2 * pr + 2) * P)
            m_parts, bt_parts, ea_parts = [], [], []
            for h in (h0, h0 + 1):
                a_bc2 = jnp.broadcast_to(a2[:, h:h + 1], (L, LANES))
                src_row = src2_t[h:h + 1, :]
                m_parts.append(jnp.concatenate(
                    [jnp.concatenate([decay_block(cb, a_bc2, src_row, bi, bj) for bj in range(n_lt)], axis=1)
                     for bi in range(n_lt)], axis=0))
                bt_parts.append((b_g_t * w_end_t[h:h + 1, :]).astype(bf16))
                ea_parts.append(jnp.exp2(a_bc2))
            xsb = xsb_scr[:, cols]
            zero = jnp.zeros_like(xsb)
            lane_b = lax.broadcasted_iota(jnp.int32, xsb.shape, 1)
            x_bd = jnp.concatenate([jnp.where(lane_b < P, xsb, zero),
                                    jnp.where(lane_b >= P, xsb, zero)], axis=0)
            y_pair = jnp.dot(jnp.concatenate(m_parts, axis=1), x_bd, preferred_element_type=f32)
            ea = jnp.where(lane < P, ea_parts[0], ea_parts[1])
            y_scr[:, cols] = y_pair + y_off[:, lcols] * ea + xs_scr[:, cols] * dskip_ref[:, cols]
            upd = jnp.dot(jnp.concatenate(bt_parts, axis=1), x_bd, preferred_element_type=f32)
            el = jnp.where(lane_row < P, exp_last[:, h0:h0 + 1], exp_last[:, h0 + 1:h0 + 2])
            st_scr[g, :, lcols] = st_g[:, lcols] * el + upd

    gw = SSM_INNER // G
    for g in range(G):
        gs = slice(g * gw, (g + 1) * gw)
        zz = z_ref[rows, gs].astype(f32)
        y = y_scr[:, gs] * _silu(zz)
        ms = jnp.mean(y * y, axis=-1, keepdims=True)
        o_ref[rows, gs] = (y * lax.rsqrt(ms + EPS) * normg_ref[:, gs]).astype(o_ref.dtype)


def _ssd(proj3, dt3, conv_w, conv_b, dt_bias, a_log, d_skip_ch, norm_g):
    b, s, _ = proj3.shape
    L = SSM_CHUNK
    assert s % L == 0
    full = lambda shape: pl.BlockSpec(shape, lambda bi, c: (0,) * len(shape))
    t_idx = np.arange(L)
    shift = np.concatenate([(t_idx[None, :] == t_idx[:, None] - k) for k in range(1, SSM_CONV)], axis=1)
    shift = jnp.asarray(shift, jnp.bfloat16)
    rows = L * SSD_CHUNKS_PER_STEP
    assert s % rows == 0
    return pl.pallas_call(
        _ssd_kernel,
        grid=(b, s // rows),
        in_specs=[
            pl.BlockSpec((None, rows, SSM_CONV_DIM), lambda bi, c: (bi, c, COL_XBC // SSM_CONV_DIM)),
            pl.BlockSpec((None, rows, SSM_INNER), lambda bi, c: (bi, c, COL_Z // SSM_INNER)),
            pl.BlockSpec((None, rows, LANES), lambda bi, c: (bi, c, 0)),
            full((L, (SSM_CONV - 1) * L)),
            full((SSM_CONV, SSM_CONV_DIM)), full((1, SSM_CONV_DIM)),
            full((1, LANES)), full((1, LANES)), full((1, SSM_INNER)), full((1, SSM_INNER)),
        ],
        out_specs=pl.BlockSpec((None, rows, SSM_INNER), lambda bi, c: (bi, c, 0)),
        out_shape=jax.ShapeDtypeStruct((b, s, SSM_INNER), jnp.bfloat16),
        scratch_shapes=[
            pltpu.VMEM((2 * SUBLANES, SSM_CONV_DIM), jnp.float32),
            pltpu.VMEM((L, SSM_INNER), jnp.float32),
            pltpu.VMEM((L, SSM_INNER), jnp.bfloat16),
            pltpu.VMEM((L, SSM_GROUPS * SSM_STATE), jnp.bfloat16),
            pltpu.VMEM((L, SSM_GROUPS * SSM_STATE), jnp.bfloat16),
            pltpu.VMEM((L, SSM_INNER), jnp.float32),
            pltpu.VMEM((SSM_GROUPS, SSM_STATE, SSM_HEADS_PER_GROUP * SSM_HEAD_DIM), jnp.float32),
        ],
        compiler_params=pltpu.CompilerParams(
            dimension_semantics=("arbitrary", "arbitrary"),
            vmem_limit_bytes=VMEM_LIMIT_BYTES),
        name="ssd",
    )(proj3, proj3, dt3, shift, conv_w, conv_b, dt_bias, a_log, d_skip_ch, norm_g)


def _out_kernel(x_ref, oatt_ref, yssm_ref, ga_ref, gs_ref, watt_ref, wssm_ref, wout_ref, gpost_ref, o_ref):
    y_att = jnp.dot(oatt_ref[...], watt_ref[...], preferred_element_type=jnp.float32)
    y_ssm = jnp.dot(yssm_ref[...], wssm_ref[...], preferred_element_type=jnp.float32)
    ga = ga_ref[...].astype(jnp.float32)
    gs = gs_ref[...].astype(jnp.float32)
    mixed = y_att / (1.0 + jnp.exp(-ga)) + y_ssm / (1.0 + jnp.exp(-gs))
    out = jnp.dot(mixed.astype(jnp.bfloat16), wout_ref[...], preferred_element_type=jnp.float32)
    ms = jnp.mean(out * out, axis=-1, keepdims=True)
    o_ref[...] = x_ref[...] + out * lax.rsqrt(ms + EPS) * gpost_ref[...]


def _out_proj(x2, o_att, y_ssm, proj2, w_att, w_ssm, w_out, g_post):
    m = x2.shape[0]
    tm = min(OUT_TM, m)
    const = lambda shape: pl.BlockSpec(shape, lambda i: (0, 0), pipeline_mode=pl.Buffered(1))
    return pl.pallas_call(
        _out_kernel,
        grid=(m // tm,),
        in_specs=[
            pl.BlockSpec((tm, D_MODEL), lambda i: (i, 0)),
            pl.BlockSpec((tm, ATT_WIDTH), lambda i: (i, 0)),
            pl.BlockSpec((tm, SSM_INNER), lambda i: (i, 0)),
            pl.BlockSpec((tm, D_MODEL), lambda i: (i, COL_GATE // D_MODEL)),
            pl.BlockSpec((tm, D_MODEL), lambda i: (i, COL_GATE // D_MODEL + 1)),
            const((ATT_WIDTH, D_MODEL)), const((SSM_INNER, D_MODEL)), const((D_MODEL, D_MODEL)),
            const((1, D_MODEL)),
        ],
        out_specs=pl.BlockSpec((tm, D_MODEL), lambda i: (i, 0)),
        out_shape=jax.ShapeDtypeStruct((m, D_MODEL), jnp.float32),
        compiler_params=pltpu.CompilerParams(
            dimension_semantics=("arbitrary",),
            vmem_limit_bytes=VMEM_LIMIT_BYTES),
        name="out_proj",
    )(x2, o_att, y_ssm, proj2, proj2, w_att, w_ssm, w_out, g_post)


def kernel(x, g_pre, w_in, att_lambda_q1, att_lambda_k1, att_lambda_q2, att_lambda_k2, att_subln_g, rel_bias,
           conv_w, conv_b, dt_bias, a_log, d_skip, ssm_norm_g, w_att_proj, w_ssm_proj, w_out, g_post):
    b, s, d = x.shape
    assert d == D_MODEL and g_pre.shape[0] == 1
    m = b * s
    f32, bf16 = jnp.float32, jnp.bfloat16
    t = min(ATT_BLOCK, s)
    assert s % t == 0 and s % SSM_CHUNK == 0

    w = w_in[0]
    o_q, o_k, o_v, o_g = 0, ATT_WIDTH, 2 * ATT_WIDTH, 3 * ATT_WIDTH
    o_z = 4 * ATT_WIDTH
    o_xbc = o_z + SSM_INNER
    o_dt = o_xbc + SSM_CONV_DIM
    o_gate = o_dt + SSM_HEADS
    w_main = jnp.concatenate([w[:, o_q:o_z], w[:, o_xbc:o_dt], w[:, o_z:o_xbc], w[:, o_gate:]], axis=1).astype(bf16)
    w_dt = jnp.pad(w[:, o_dt:o_gate], ((0, 0), (0, LANES - SSM_HEADS))).astype(bf16)
    col_scale = jnp.concatenate([jnp.full((1, ATT_WIDTH), ATT_QK_DIM ** -0.5 * LOG2E, f32),
                                 jnp.ones((1, PROJ_COLS - ATT_WIDTH), f32)], axis=1)

    x2 = x.reshape(m, d)
    proj, dt_raw = _in_proj(x2, g_pre, w_main, w_dt, col_scale)
    proj3 = proj.reshape(b, s, PROJ_COLS)
    dt3 = dt_raw.reshape(b, s, LANES)

    bias = _bias_tiles(rel_bias, t)
    o_att = _attention(proj3, bias, att_subln_g, att_lambda_q1, att_lambda_k1, att_lambda_q2, att_lambda_k2, t)

    pad_h = lambda v: jnp.pad(v, ((0, 0), (0, LANES - SSM_HEADS)))
    d_skip_ch = jnp.repeat(d_skip, SSM_HEAD_DIM, axis=1)
    y_ssm = _ssd(proj3, dt3, conv_w[0], conv_b, pad_h(dt_bias), pad_h(a_log), d_skip_ch, ssm_norm_g)

    out = _out_proj(x2, o_att.reshape(m, ATT_WIDTH), y_ssm.reshape(m, SSM_INNER), proj,
                    w_att_proj[0].astype(bf16), w_ssm_proj[0].astype(bf16), w_out[0].astype(bf16), g_post)
    return out.reshape(b, s, d)
```
